```python
import math
import jax, jax.numpy as jnp
from jax import lax
import numpy as np

D_MODEL = 1024
BATCH = 8
SEQ = 2048
DEPTH = 1
DEC_BATCH = 128
DEC_SEQ = 1
PAST_LEN = 8192
PAGE_SIZE = 128

D_MIX = D_MODEL
ATT_HEAD_DIM = 64
ATT_V_DIM = 2 * ATT_HEAD_DIM
D_ATT = D_MIX // 2
N_ATT_HEADS = D_ATT // ATT_V_DIM
D_QK = N_ATT_HEADS * 2 * ATT_HEAD_DIM
D_SSM = D_MIX - D_ATT
SSM_GROUP = 16
N_SSM_GROUPS = D_SSM // SSM_GROUP
SSM_STATE = 64
D_IN = 2 * D_QK + D_ATT + D_SSM
D_FF = ((8 * D_MODEL // 3 + 127) // 128) * 128
CONV_WIDTH = 3
Q_BLOCK = 128
EPS = 1e-6
DT_MIN = 1e-3
DT_MAX = 1e-1

kernel_name = "hymba_diffattn_s5_convffn_step"


def rms_norm(x, g):
    xf = x.astype(jnp.float32)
    y = xf * lax.rsqrt(jnp.mean(xf * xf, axis=-1, keepdims=True) + EPS)
    return (y * g.astype(jnp.float32)).astype(x.dtype)


def alibi_slopes():
    return jnp.exp2(-8.0 * jnp.arange(1, N_ATT_HEADS + 1, dtype=jnp.float32) / N_ATT_HEADS)


def lambda_init_fn(layer):
    return 0.8 - 0.6 * math.exp(-0.3 * layer)


def diff_lambda(lq1, lk1, lq2, lk2, lam_init):
    f32 = jnp.float32
    return (jnp.exp(jnp.sum(lq1.astype(f32) * lk1.astype(f32)))
            - jnp.exp(jnp.sum(lq2.astype(f32) * lk2.astype(f32))) + lam_init)


def diff_attend(q, k, v, q_pos, k_pos, lam, slopes):
    s = jnp.einsum('...thcd,...shcd->...hcts', q, k).astype(jnp.float32) * (ATT_HEAD_DIM ** -0.5)
    dist = q_pos[:, None] - k_pos[None, :]
    s = s - slopes[:, None, None, None] * dist.astype(jnp.float32)
    s = jnp.where(dist >= 0, s, -jnp.inf)
    p = jax.nn.softmax(s, axis=-1)
    a = p[..., 0, :, :] - lam * p[..., 1, :, :]
    return jnp.einsum('...hts,...shv->...thv', a.astype(v.dtype), v)


def prompt_attention(q, k, v, lam, slopes):
    b, s = q.shape[0], q.shape[1]
    nb = s // Q_BLOCK
    qb = q.reshape(b, nb, Q_BLOCK, N_ATT_HEADS, 2, ATT_HEAD_DIM).swapaxes(0, 1)
    starts = jnp.arange(nb, dtype=jnp.int32) * Q_BLOCK
    k_pos = jnp.arange(s, dtype=jnp.int32)

    def block(args):
        q_blk, start = args
        q_pos = start + jnp.arange(Q_BLOCK, dtype=jnp.int32)
        return diff_attend(q_blk, k, v, q_pos, k_pos, lam, slopes)

    o = lax.map(block, (qb, starts))
    return o.swapaxes(0, 1).reshape(b, s, N_ATT_HEADS, ATT_V_DIM)


def sample_attention(q, k_new, v_new, lam, slopes, cache_k, cache_v, page_table, layer):
    def per_seq(args):
        q_b, kn_b, vn_b, pt_b = args
        kp = cache_k[layer, pt_b].reshape(-1, N_ATT_HEADS, 2, ATT_HEAD_DIM)
        vp = cache_v[layer, pt_b].reshape(-1, N_ATT_HEADS, ATT_V_DIM)
        past = kp.shape[0]
        t = q_b.shape[0]
        kk = jnp.concatenate([kp, kn_b.astype(kp.dtype)], axis=0)
        vv = jnp.concatenate([vp, vn_b.astype(vp.dtype)], axis=0)
        k_pos = jnp.arange(past + t, dtype=jnp.int32)
        q_pos = past + jnp.arange(t, dtype=jnp.int32)
        return diff_attend(q_b, kk, vv, q_pos, k_pos, lam, slopes)

    return lax.map(per_seq, (q, k_new, v_new, page_table))


def ssm_mixer(u, h0_re, h0_im, lam_re, lam_im, log_dt, b_re, b_im, c_re, c_im, d_skip, w_glu, b_glu):
    f32 = jnp.float32
    n, t = u.shape[0], u.shape[1]
    lr, li = lam_re.astype(f32), lam_im.astype(f32)
    dt = jnp.exp(log_dt.astype(f32))[:, None]
    mag = jnp.exp(lr * dt)
    a_re, a_im = mag * jnp.cos(li * dt), mag * jnp.sin(li * dt)
    den = lr * lr + li * li
    nr, ni = a_re - 1.0, a_im
    coef_re = (nr * lr + ni * li) / den
    coef_im = (ni * lr - nr * li) / den
    br, bi = b_re.astype(f32), b_im.astype(f32)
    bb_re = coef_re[..., None] * br - coef_im[..., None] * bi
    bb_im = coef_re[..., None] * bi + coef_im[..., None] * br
    uf = u.astype(f32)
    ug = uf.reshape(n, t, N_SSM_GROUPS, SSM_GROUP)
    x_re = jnp.einsum('gpk,ntgk->ntgp', bb_re, ug)
    x_im = jnp.einsum('gpk,ntgk->ntgp', bb_im, ug)
    hr0, hi0 = h0_re.astype(f32), h0_im.astype(f32)
    x_re = x_re.at[:, 0].add(a_re * hr0 - a_im * hi0)
    x_im = x_im.at[:, 0].add(a_re * hi0 + a_im * hr0)
    a_re_t = jnp.broadcast_to(a_re, (1, t) + a_re.shape)
    a_im_t = jnp.broadcast_to(a_im, (1, t) + a_im.shape)

    def combine(e1, e2):
        a1r, a1i, b1r, b1i = e1
        a2r, a2i, b2r, b2i = e2
        return (a1r * a2r - a1i * a2i, a1r * a2i + a1i * a2r,
                a2r * b1r - a2i * b1i + b2r, a2r * b1i + a2i * b1r + b2i)

    _, _, h_re, h_im = lax.associative_scan(combine, (a_re_t, a_im_t, x_re, x_im), axis=1)
    y = (jnp.einsum('gkp,ntgp->ntgk', c_re.astype(f32), h_re)
         - jnp.einsum('gkp,ntgp->ntgk', c_im.astype(f32), h_im)).reshape(n, t, D_SSM)
    y = y + d_skip.astype(f32) * uf
    g = jax.nn.gelu(y)
    out = g * jax.nn.sigmoid(g @ w_glu.astype(f32) + b_glu.astype(f32))
    return out.astype(u.dtype), h_re[:, -1].astype(h0_re.dtype), h_im[:, -1].astype(h0_im.dtype)


def conv_ffn(x, prev, w_up, conv_w, conv_b, w_down):
    t = x.shape[1]
    h = x @ w_up
    gate, up = h[..., :D_FF], h[..., D_FF:]
    pad = jnp.concatenate([prev.astype(gate.dtype), gate], axis=1)
    gc = conv_b + sum(conv_w[j] * pad[:, j:j + t] for j in range(CONV_WIDTH))
    out = (jax.nn.silu(gc) * up) @ w_down
    return out, pad[:, -(CONV_WIDTH - 1):]


def layer_step(x, attend, h0_re, h0_im, conv_prev, p, lam_init):
    lead = x.shape[:2]
    proj = rms_norm(x, p['norm1_g']) @ p['w_in']
    q = proj[..., :D_QK].reshape(*lead, N_ATT_HEADS, 2, ATT_HEAD_DIM)
    k = proj[..., D_QK:2 * D_QK].reshape(*lead, N_ATT_HEADS, 2, ATT_HEAD_DIM)
    v = proj[..., 2 * D_QK:2 * D_QK + D_ATT].reshape(*lead, N_ATT_HEADS, ATT_V_DIM)
    u = proj[..., 2 * D_QK + D_ATT:]
    lam = diff_lambda(p['lam_q1'], p['lam_k1'], p['lam_q2'], p['lam_k2'], lam_init)
    o = attend(q, k, v, lam)
    o = (rms_norm(o, p['subln_g']) * (1.0 - lam_init)).reshape(*lead, D_ATT)
    s, h_re, h_im = ssm_mixer(u, h0_re, h0_im, p['ssm_lam_re'], p['ssm_lam_im'], p['ssm_log_dt'],
                              p['ssm_b_re'], p['ssm_b_im'], p['ssm_c_re'], p['ssm_c_im'],
                              p['ssm_d'], p['w_glu'], p['b_glu'])
    x = x + jnp.concatenate([o, s.astype(o.dtype)], axis=-1) @ p['w_out']
    f, conv_state = conv_ffn(rms_norm(x, p['norm2_g']), conv_prev,
                             p['w_up'], p['conv_w'], p['conv_b'], p['w_down'])
    return x + f, k, v, h_re, h_im, conv_state


def setup_inputs(seed: int = 0) -> dict:
    key = jax.random.key(seed)
    ks = iter(jax.random.split(key, 40))
    nrm = lambda shape, scale: jax.random.normal(next(ks), shape, jnp.float32) * scale
    n_pages = PAST_LEN // PAGE_SIZE
    used = DEC_BATCH * n_pages
    n_pool = used + max(1, used // 4)
    G, P, C = N_SSM_GROUPS, SSM_STATE, SSM_GROUP
    page_table = jax.random.permutation(next(ks), n_pool)[:used].reshape(DEC_BATCH, n_pages).astype(jnp.int32)
    lam_im0 = math.pi * jnp.arange(P, dtype=jnp.float32)
    return {
        "x_prompt": nrm((BATCH, SEQ, D_MODEL), 1.0),
        "x_sample": nrm((DEC_BATCH, DEC_SEQ, D_MODEL), 1.0),
        "cache_k": nrm((DEPTH, n_pool, PAGE_SIZE, N_ATT_HEADS, 2, ATT_HEAD_DIM), 1.0),
        "cache_v": nrm((DEPTH, n_pool, PAGE_SIZE, N_ATT_HEADS, ATT_V_DIM), 1.0),
        "state_ssm_re": nrm((DEPTH, DEC_BATCH, G, P), 0.5),
        "state_ssm_im": nrm((DEPTH, DEC_BATCH, G, P), 0.5),
        "cache_ffn_conv": nrm((DEPTH, DEC_BATCH, CONV_WIDTH - 1, D_FF), 1.0),
        "page_table": page_table,
        "norm1_g": 1.0 + nrm((DEPTH, D_MODEL), 0.02),
        "w_in": nrm((DEPTH, D_MODEL, D_IN), D_MODEL ** -0.5),
        "lam_q1": nrm((DEPTH, ATT_HEAD_DIM), 0.1),
        "lam_k1": nrm((DEPTH, ATT_HEAD_DIM), 0.1),
        "lam_q2": nrm((DEPTH, ATT_HEAD_DIM), 0.1),
        "lam_k2": nrm((DEPTH, ATT_HEAD_DIM), 0.1),
        "subln_g": 1.0 + nrm((DEPTH, ATT_V_DIM), 0.02),
        "ssm_lam_re": -0.5 + nrm((DEPTH, G, P), 0.01),
        "ssm_lam_im": lam_im0 + nrm((DEPTH, G, P), 0.01),
        "ssm_log_dt": jax.random.uniform(next(ks), (DEPTH, G), jnp.float32, math.log(DT_MIN), math.log(DT_MAX)),
        "ssm_b_re": nrm((DEPTH, G, P, C), (2 * C) ** -0.5),
        "ssm_b_im": nrm((DEPTH, G, P, C), (2 * C) ** -0.5),
        "ssm_c_re": nrm((DEPTH, G, C, P), (2 * P) ** -0.5),
        "ssm_c_im": nrm((DEPTH, G, C, P), (2 * P) ** -0.5),
        "ssm_d": nrm((DEPTH, D_SSM), 1.0),
        "w_glu": nrm((DEPTH, D_SSM, D_SSM), D_SSM ** -0.5),
        "b_glu": nrm((DEPTH, D_SSM), 0.01),
        "w_out": nrm((DEPTH, D_MIX, D_MODEL), D_MIX ** -0.5),
        "norm2_g": 1.0 + nrm((DEPTH, D_MODEL), 0.02),
        "w_up": nrm((DEPTH, D_MODEL, 2 * D_FF), D_MODEL ** -0.5),
        "conv_w": nrm((DEPTH, CONV_WIDTH, D_FF), CONV_WIDTH ** -0.5),
        "conv_b": nrm((DEPTH, D_FF), 0.01),
        "w_down": nrm((DEPTH, D_FF, D_MODEL), D_FF ** -0.5),
        "final_g": 1.0 + nrm((D_MODEL,), 0.02),
    }


def reference(x_prompt, x_sample, cache_k, cache_v, state_ssm_re, state_ssm_im, cache_ffn_conv,
              page_table, norm1_g, w_in, lam_q1, lam_k1, lam_q2, lam_k2, subln_g,
              ssm_lam_re, ssm_lam_im, ssm_log_dt, ssm_b_re, ssm_b_im, ssm_c_re, ssm_c_im,
              ssm_d, w_glu, b_glu, w_out, norm2_g, w_up, conv_w, conv_b, w_down, final_g):
    slopes = alibi_slopes()
    xp, xs = x_prompt, x_sample
    kp_l, vp_l, ks_l, vs_l = [], [], [], []
    hrp_l, hip_l, hrs_l, his_l, cp_l, cs_l = [], [], [], [], [], []
    for l in range(DEPTH):
        p = dict(norm1_g=norm1_g[l], w_in=w_in[l], lam_q1=lam_q1[l], lam_k1=lam_k1[l],
                 lam_q2=lam_q2[l], lam_k2=lam_k2[l], subln_g=subln_g[l],
                 ssm_lam_re=ssm_lam_re[l], ssm_lam_im=ssm_lam_im[l], ssm_log_dt=ssm_log_dt[l],
                 ssm_b_re=ssm_b_re[l], ssm_b_im=ssm_b_im[l], ssm_c_re=ssm_c_re[l], ssm_c_im=ssm_c_im[l],
                 ssm_d=ssm_d[l], w_glu=w_glu[l], b_glu=b_glu[l], w_out=w_out[l], norm2_g=norm2_g[l],
                 w_up=w_up[l], conv_w=conv_w[l], conv_b=conv_b[l], w_down=w_down[l])
        lam_init = lambda_init_fn(l)
        zeros_h = jnp.zeros((xp.shape[0], N_SSM_GROUPS, SSM_STATE), state_ssm_re.dtype)
        zeros_c = jnp.zeros((xp.shape[0], CONV_WIDTH - 1, D_FF), cache_ffn_conv.dtype)
        attend_p = lambda q, k, v, lam: prompt_attention(q, k, v, lam, slopes)
        xp, kp, vp, hrp, hip, cp = layer_step(xp, attend_p, zeros_h, zeros_h, zeros_c, p, lam_init)
        attend_s = lambda q, k, v, lam, _l=l: sample_attention(q, k, v, lam, slopes, cache_k, cache_v, page_table, _l)
        xs, ks_, vs_, hrs, his, cs = layer_step(xs, attend_s, state_ssm_re[l], state_ssm_im[l],
                                                cache_ffn_conv[l], p, lam_init)
        kp_l.append(kp); vp_l.append(vp); ks_l.append(ks_); vs_l.append(vs_)
        hrp_l.append(hrp); hip_l.append(hip); hrs_l.append(hrs); his_l.append(his)
        cp_l.append(cp); cs_l.append(cs)
    y_prompt = rms_norm(xp, final_g)
    y_sample = rms_norm(xs, final_g)
    return (y_prompt, y_sample,
            jnp.stack(kp_l), jnp.stack(vp_l), jnp.stack(ks_l), jnp.stack(vs_l),
            jnp.stack(hrp_l), jnp.stack(hip_l), jnp.stack(hrs_l), jnp.stack(his_l),
            jnp.stack(cp_l), jnp.stack(cs_l))
```

```python
import functools
import math

import jax
import jax.numpy as jnp
from jax import lax
from jax.experimental import pallas as pl
from jax.experimental.pallas import tpu as pltpu

F32 = jnp.float32
BF16 = jnp.bfloat16

D_MODEL = 1024
N_HEADS = 4
HEAD_DIM = 64
V_DIM = 2 * HEAD_DIM
D_ATT = N_HEADS * V_DIM
D_QK = N_HEADS * 2 * HEAD_DIM
D_SSM = 512
SSM_GROUP = 16
N_GROUPS = D_SSM // SSM_GROUP
SSM_STATE = 64
N_STATE = N_GROUPS * SSM_STATE
D_IN = 2 * D_QK + D_ATT + D_SSM
D_FF = 2816
PAGE = 128
EPS = 1e-6
LAM_INIT = 0.8 - 0.6 * math.exp(-0.3 * 0)
QK_SCALE = HEAD_DIM ** -0.5

LANES = 128
SUBLANES = 8
MXU_COLS = 256
VMEM_LIMIT = 56 * 1024 * 1024

FF_CHUNK = MXU_COLS
N_FF_CHUNKS = D_FF // FF_CHUNK
SSM_CHUNK_GROUPS = MXU_COLS // SSM_GROUP
SSM_CHUNK_STATES = SSM_CHUNK_GROUPS * SSM_STATE
N_SSM_CHUNKS = D_SSM // MXU_COLS
SCAN_COLS = 512


def _params(n_axes, vmem=VMEM_LIMIT):
    return pltpu.CompilerParams(dimension_semantics=("arbitrary",) * n_axes, vmem_limit_bytes=vmem)


def _rms(x, g):
    return x * lax.rsqrt(jnp.mean(x * x, axis=-1, keepdims=True) + EPS) * g


def _sigmoid(x):
    return 1.0 / (1.0 + jnp.exp(-x))


def _diff_lambda(lamp_ref):
    lp = lamp_ref[...]
    a1 = jnp.sum(lp[0:1] * lp[1:2], axis=-1, keepdims=True)
    a2 = jnp.sum(lp[2:3] * lp[3:4], axis=-1, keepdims=True)
    return jnp.exp(a1) - jnp.exp(a2) + LAM_INIT


def _resident(shape):
    return pl.BlockSpec(shape, lambda *_: (0,) * len(shape), pipeline_mode=pl.Buffered(1))


def _disc_body(lr_ref, li_ref, ldt_ref, br_ref, bi_ref, ar_ref, ai_ref, bbr_ref, bbi_ref):
    lr, li = lr_ref[...], li_ref[...]
    dt = jnp.exp(ldt_ref[...])
    mag = jnp.exp(lr * dt)
    a_re, a_im = mag * jnp.cos(li * dt), mag * jnp.sin(li * dt)
    den = lr * lr + li * li
    nr, ni = a_re - 1.0, a_im
    coef_re = (nr * lr + ni * li) / den
    coef_im = (ni * lr - nr * li) / den
    br, bi = br_ref[...], bi_ref[...]
    bbr_ref[...] = coef_re * br - coef_im * bi
    bbi_ref[...] = coef_re * bi + coef_im * br
    ar_ref[...] = a_re
    ai_ref[...] = a_im


def _ssm_discretize(lam_re, lam_im, log_dt, b_re, b_im):
    row = jax.ShapeDtypeStruct((1, N_STATE), F32)
    mat = jax.ShapeDtypeStruct((SSM_GROUP, N_STATE), F32)
    return pl.pallas_call(_disc_body, out_shape=(row, row, mat, mat), name="ssm_discretize")(
        lam_re.reshape(1, N_STATE), lam_im.reshape(1, N_STATE),
        jnp.repeat(log_dt, SSM_STATE).reshape(1, N_STATE),
        b_re.transpose(2, 0, 1).reshape(SSM_GROUP, N_STATE),
        b_im.transpose(2, 0, 1).reshape(SSM_GROUP, N_STATE))


def _blockdiag_in(bbt):
    t = bbt.reshape(SSM_GROUP, N_SSM_CHUNKS, SSM_CHUNK_GROUPS, SSM_STATE)
    eye = jnp.eye(SSM_CHUNK_GROUPS, dtype=bbt.dtype)
    return jnp.einsum('cngp,gh->ngchp', t, eye).reshape(N_SSM_CHUNKS, MXU_COLS, SSM_CHUNK_STATES)


def _blockdiag_out(c):
    t = c.reshape(N_SSM_CHUNKS, SSM_CHUNK_GROUPS, SSM_GROUP, SSM_STATE)
    eye = jnp.eye(SSM_CHUNK_GROUPS, dtype=c.dtype)
    return jnp.einsum('ngkp,gh->ngphk', t, eye).reshape(N_SSM_CHUNKS, SSM_CHUNK_STATES, MXU_COLS)


def _inproj_body(x_ref, g_ref, w_ref, q_ref, k_ref, v_ref, u_ref, *maybe_bf16_kv):
    xn = _rms(x_ref[...], g_ref[...]).astype(BF16)

    def proj(lo):
        return jnp.dot(xn, w_ref[:, lo:lo + D_QK], preferred_element_type=F32)

    q_ref[...] = (proj(0) * QK_SCALE).astype(q_ref.dtype)
    k = proj(D_QK)
    v = proj(2 * D_QK)
    k_ref[...] = k
    v_ref[...] = v
    u_ref[...] = proj(2 * D_QK + D_ATT)
    if maybe_bf16_kv:
        kb_ref, vb_ref = maybe_bf16_kv
        kb_ref[...] = k.astype(BF16)
        vb_ref[...] = v.astype(BF16)


def _in_proj(x, g, w_bf, *, tm, q_dtype, bf16_kv):
    b, s, _ = x.shape
    tok = lambda dt: jax.ShapeDtypeStruct((b, s, D_QK), dt)
    tok_spec = pl.BlockSpec((None, tm, D_QK), lambda i, j: (i, j, 0))
    out_shape = [tok(q_dtype), tok(F32), tok(F32), jax.ShapeDtypeStruct((s, b * D_SSM), F32)]
    out_specs = [tok_spec, tok_spec, tok_spec, pl.BlockSpec((tm, D_SSM), lambda i, j: (j, i))]
    if bf16_kv:
        out_shape += [tok(BF16), tok(BF16)]
        out_specs += [tok_spec, tok_spec]
    return pl.pallas_call(
        _inproj_body,
        grid=(b, s // tm),
        in_specs=[pl.BlockSpec((None, tm, D_MODEL), lambda i, j: (i, j, 0)),
                  _resident((1, D_MODEL)), _resident((D_MODEL, D_IN))],
        out_specs=out_specs, out_shape=out_shape,
        compiler_params=_params(2), name="in_proj")(x, g, w_bf)


def _prompt_attn_body(lamp_ref, sg_ref, q_ref, k_ref, v_ref, o_ref, *, tq):
    qi = pl.program_id(1)
    lam = _diff_lambda(lamp_ref)
    lane = lax.broadcasted_iota(jnp.int32, (1, V_DIM), 1)
    rel = (lax.broadcasted_iota(jnp.int32, (tq, tq), 0)
           - lax.broadcasted_iota(jnp.int32, (tq, tq), 1))
    causal = rel >= 0
    rel_f = rel.astype(F32)
    nt = (((1,), (1,)), ((), ()))

    for h in range(N_HEADS):
        slope = 2.0 ** (-8.0 * (h + 1) / N_HEADS)
        hs = slice(h * V_DIM, (h + 1) * V_DIM)
        q = q_ref[:, hs]
        qz = (jnp.where(lane < HEAD_DIM, q, jnp.zeros_like(q)),
              jnp.where(lane >= HEAD_DIM, q, jnp.zeros_like(q)))
        base = -slope * rel_f

        def block(j, carry, masked, qz=qz, base=base, slope=slope, hs=hs):
            rows = pl.ds(pl.multiple_of(j * tq, tq), tq)
            k = k_ref[rows, hs]
            v = v_ref[rows, hs]
            off = -slope * ((qi - j) * tq).astype(F32)
            new = []
            for c in range(2):
                m, l, acc = carry[c]
                s = lax.dot_general(qz[c], k, nt, preferred_element_type=F32) + base
                if masked:
                    s = jnp.where(causal, s, -jnp.inf)
                m_new = jnp.maximum(m, jnp.max(s, axis=-1, keepdims=True) + off)
                alpha = jnp.exp(m - m_new)
                p = jnp.exp(s - (m_new - off))
                l = alpha * l + jnp.sum(p, axis=-1, keepdims=True)
                acc = alpha * acc + jnp.dot(p.astype(BF16), v, preferred_element_type=F32)
                new.append((m_new, l, acc))
            return tuple(new)

        init = tuple((jnp.full((tq, 1), -jnp.inf, F32), jnp.zeros((tq, 1), F32),
                      jnp.zeros((tq, V_DIM), F32)) for _ in range(2))
        carry = lax.fori_loop(0, qi, functools.partial(block, masked=False), init)
        (_, l1, a1), (_, l2, a2) = block(qi, carry, masked=True)
        o = a1 / l1 - lam * (a2 / l2)
        o_ref[:, hs] = (_rms(o, sg_ref[...]) * (1.0 - LAM_INIT)).astype(o_ref.dtype)


def _prompt_attention(lamp, subln_g, q, k, v, *, tq):
    b, s, _ = q.shape
    seq_spec = pl.BlockSpec((None, s, D_QK), lambda i, j: (i, 0, 0))
    return pl.pallas_call(
        functools.partial(_prompt_attn_body, tq=tq),
        grid=(b, s // tq),
        in_specs=[_resident((4, HEAD_DIM)), _resident((1, V_DIM)),
                  pl.BlockSpec((None, tq, D_QK), lambda i, j: (i, j, 0)), seq_spec, seq_spec],
        out_specs=pl.BlockSpec((None, tq, D_ATT), lambda i, j: (i, j, 0)),
        out_shape=jax.ShapeDtypeStruct((b, s, D_ATT), BF16),
        compiler_params=_params(2), name="prompt_attn")(lamp, subln_g, q, k, v)


def _decode_attn_body(pt_ref, lamp_ref, sg_ref, q_ref, kn_ref, vn_ref, ck_hbm, cv_hbm, o_ref,
                      kbuf, vbuf, sem, m_ref, l_ref, acc_ref, *, n_chunks, pages_per_chunk, n_pages):
    b, c = pl.program_id(0), pl.program_id(1)
    step = b * n_chunks + c
    n_steps = pl.num_programs(0) * n_chunks
    slot = step % 2
    chunk_tokens = pages_per_chunk * PAGE
    past_len = n_pages * PAGE

    def page_copies(seq, chunk, slot_):
        copies = []
        for p in range(pages_per_chunk):
            page = pt_ref[seq * n_pages + chunk * pages_per_chunk + p]
            rows = pl.ds(p * PAGE, PAGE)
            copies.append(pltpu.make_async_copy(ck_hbm.at[page], kbuf.at[slot_, rows, :], sem.at[0, slot_]))
            copies.append(pltpu.make_async_copy(cv_hbm.at[page], vbuf.at[slot_, rows, :], sem.at[1, slot_]))
        return copies

    @pl.when(step == 0)
    def _():
        for cp in page_copies(0, 0, 0):
            cp.start()

    @pl.when(step + 1 < n_steps)
    def _():
        nxt = step + 1
        for cp in page_copies(nxt // n_chunks, nxt % n_chunks, 1 - slot):
            cp.start()

    @pl.when(c == 0)
    def _():
        m_ref[...] = jnp.full(m_ref.shape, -jnp.inf, F32)
        l_ref[...] = jnp.zeros(l_ref.shape, F32)
        acc_ref[...] = jnp.zeros(acc_ref.shape, F32)

    n_maps = 2 * N_HEADS
    row = lax.broadcasted_iota(jnp.int32, (n_maps, D_QK), 0)
    lane = lax.broadcasted_iota(jnp.int32, (n_maps, D_QK), 1)
    qblk = jnp.where(lane // HEAD_DIM == row, jnp.broadcast_to(q_ref[...], (n_maps, D_QK)), 0.0)
    row1 = lax.broadcasted_iota(jnp.int32, (n_maps, 1), 0)
    slope = jnp.exp2(-8.0 * ((row1 // 2) + 1).astype(F32) / N_HEADS)

    for cp in page_copies(b, c, slot):
        cp.wait()

    kc = kbuf[slot].astype(BF16)
    vc = vbuf[slot].astype(BF16)
    s = lax.dot_general(qblk.astype(BF16), kc, (((1,), (1,)), ((), ())), preferred_element_type=F32)
    pos = c * chunk_tokens + lax.broadcasted_iota(jnp.int32, (1, chunk_tokens), 1)
    s = s - slope * (past_len - pos).astype(F32)
    m_old = m_ref[...]
    m_new = jnp.maximum(m_old, jnp.max(s, axis=-1, keepdims=True))
    alpha = jnp.exp(m_old - m_new)
    p = jnp.exp(s - m_new)
    l_ref[...] = alpha * l_ref[...] + jnp.sum(p, axis=-1, keepdims=True)
    acc_ref[...] = alpha * acc_ref[...] + jnp.dot(p.astype(BF16), vc, preferred_element_type=F32)
    m_ref[...] = m_new

    @pl.when(c == n_chunks - 1)
    def _():
        s_new = jnp.sum(qblk * kn_ref[...], axis=-1, keepdims=True)
        m_old = m_ref[...]
        m_fin = jnp.maximum(m_old, s_new)
        alpha = jnp.exp(m_old - m_fin)
        p_new = jnp.exp(s_new - m_fin)
        l_fin = alpha * l_ref[...] + p_new
        acc = alpha * acc_ref[...] + p_new * vn_ref[...]
        lam = _diff_lambda(lamp_ref)
        coef = jnp.where(row1 % 2 == 0, 1.0, -lam) / l_fin
        own = lane // V_DIM == row // 2
        o = jnp.sum(jnp.where(own, coef * acc, 0.0), axis=0, keepdims=True)
        for h in range(N_HEADS):
            hs = slice(h * V_DIM, (h + 1) * V_DIM)
            o_ref[:, hs] = _rms(o[:, hs], sg_ref[...]) * (1.0 - LAM_INIT)


def _decode_attention(page_table, lamp, subln_g, q, k_new, v_new, cache_k, cache_v, *, pages_per_chunk):
    n, n_pages = page_table.shape
    n_chunks = n_pages // pages_per_chunk
    chunk_tokens = pages_per_chunk * PAGE
    row_spec = pl.BlockSpec((None, 1, D_QK), lambda i, j, pt: (i, 0, 0))
    body = functools.partial(_decode_attn_body, n_chunks=n_chunks, pages_per_chunk=pages_per_chunk,
                             n_pages=n_pages)
    grid_spec = pltpu.PrefetchScalarGridSpec(
        num_scalar_prefetch=1,
        grid=(n, n_chunks),
        in_specs=[pl.BlockSpec((4, HEAD_DIM), lambda i, j, pt: (0, 0)),
                  pl.BlockSpec((1, V_DIM), lambda i, j, pt: (0, 0)),
                  row_spec, row_spec, row_spec,
                  pl.BlockSpec(memory_space=pl.ANY), pl.BlockSpec(memory_space=pl.ANY)],
        out_specs=row_spec,
        scratch_shapes=[pltpu.VMEM((2, chunk_tokens, D_QK), F32),
                        pltpu.VMEM((2, chunk_tokens, D_ATT), F32),
                        pltpu.SemaphoreType.DMA((2, 2)),
                        pltpu.VMEM((2 * N_HEADS, 1), F32),
                        pltpu.VMEM((2 * N_HEADS, 1), F32),
                        pltpu.VMEM((2 * N_HEADS, D_ATT), F32)])
    return pl.pallas_call(
        body, grid_spec=grid_spec, out_shape=jax.ShapeDtypeStruct((n, 1, D_ATT), F32),
        compiler_params=_params(2), name="decode_attn")(
            page_table.reshape(-1), lamp, subln_g, q, k_new, v_new, cache_k, cache_v)


def _ssm_body(u_ref, h0_ref, ar_ref, ai_ref, bre_ref, bim_ref, cre_ref, cim_ref, d_ref, wg_ref, bg_ref,
              s_ref, h_ref, x_scr, *, rows_per_step, n_steps):
    r = rows_per_step

    @pl.when(pl.program_id(0) == 0)
    def _():
        h_ref[...] = h0_ref[...]

    u = u_ref[...]
    ub = u.astype(BF16)
    for n in range(N_SSM_CHUNKS):
        un = ub[:, n * MXU_COLS:(n + 1) * MXU_COLS]
        cols = slice(n * SSM_CHUNK_STATES, (n + 1) * SSM_CHUNK_STATES)
        x_scr[:, cols] = jnp.dot(un, bre_ref[n], preferred_element_type=F32)
        x_scr[:, N_STATE + n * SSM_CHUNK_STATES:N_STATE + (n + 1) * SSM_CHUNK_STATES] = jnp.dot(
            un, bim_ref[n], preferred_element_type=F32)

    for cb in range(N_STATE // SCAN_COLS):
        re = slice(cb * SCAN_COLS, (cb + 1) * SCAN_COLS)
        im = slice(N_STATE + cb * SCAN_COLS, N_STATE + (cb + 1) * SCAN_COLS)
        a_r = jnp.broadcast_to(ar_ref[:, re], (r, SCAN_COLS))
        a_i = jnp.broadcast_to(ai_ref[:, re], (r, SCAN_COLS))

        def step(t, carry, re=re, im=im, a_r=a_r, a_i=a_i):
            hr, hi = carry
            rows = pl.ds(pl.multiple_of(t * r, r), r)
            nhr = a_r * hr - a_i * hi + x_scr[rows, re]
            nhi = a_r * hi + a_i * hr + x_scr[rows, im]
            x_scr[rows, re] = nhr
            x_scr[rows, im] = nhi
            return nhr, nhi

        carry = (h_ref[:, re], h_ref[:, im])
        if n_steps == 1:
            hr, hi = step(0, carry)
        else:
            hr, hi = lax.fori_loop(0, n_steps, step, carry, unroll=8)
        h_ref[:, re] = hr
        h_ref[:, im] = hi

    ys = []
    for n in range(N_SSM_CHUNKS):
        cols = slice(n * SSM_CHUNK_STATES, (n + 1) * SSM_CHUNK_STATES)
        cols_im = slice(N_STATE + n * SSM_CHUNK_STATES, N_STATE + (n + 1) * SSM_CHUNK_STATES)
        ys.append(jnp.dot(x_scr[:, cols].astype(BF16), cre_ref[n], preferred_element_type=F32)
                  - jnp.dot(x_scr[:, cols_im].astype(BF16), cim_ref[n], preferred_element_type=F32))
    y = jnp.concatenate(ys, axis=-1) + d_ref[...] * u
    g = 0.5 * y * (1.0 + jnp.tanh(math.sqrt(2.0 / math.pi) * (y + 0.044715 * (y * y * y))))
    z = jnp.dot(g.astype(BF16), wg_ref[...], preferred_element_type=F32) + bg_ref[...]
    s_ref[...] = (g * _sigmoid(z)).astype(s_ref.dtype)


def _ssm_glu(u, h0, a_re, a_im, b_re_bd, b_im_bd, c_re_bd, c_im_bd, d_skip, w_glu_bf, b_glu,
             *, rows_per_step, steps_per_block):
    n_rows = u.shape[0]
    blk = rows_per_step * steps_per_block
    body = functools.partial(_ssm_body, rows_per_step=rows_per_step, n_steps=steps_per_block)
    return pl.pallas_call(
        body,
        grid=(n_rows // blk,),
        in_specs=[pl.BlockSpec((blk, D_SSM), lambda i: (i, 0)),
                  _resident((rows_per_step, 2 * N_STATE)),
                  _resident((1, N_STATE)), _resident((1, N_STATE)),
                  _resident((N_SSM_CHUNKS, MXU_COLS, SSM_CHUNK_STATES)),
                  _resident((N_SSM_CHUNKS, MXU_COLS, SSM_CHUNK_STATES)),
                  _resident((N_SSM_CHUNKS, SSM_CHUNK_STATES, MXU_COLS)),
                  _resident((N_SSM_CHUNKS, SSM_CHUNK_STATES, MXU_COLS)),
                  _resident((1, D_SSM)), _resident((D_SSM, D_SSM)), _resident((1, D_SSM))],
        out_specs=[pl.BlockSpec((blk, D_SSM), lambda i: (i, 0)),
                   pl.BlockSpec((rows_per_step, 2 * N_STATE), lambda i: (0, 0))],
        out_shape=[jax.ShapeDtypeStruct((n_rows, D_SSM), BF16),
                   jax.ShapeDtypeStruct((rows_per_step, 2 * N_STATE), F32)],
        scratch_shapes=[pltpu.VMEM((blk, 2 * N_STATE), F32)],
        compiler_params=_params(1), name="ssm_glu")(
            u, h0, a_re, a_im, b_re_bd, b_im_bd, c_re_bd, c_im_bd, d_skip, w_glu_bf, b_glu)


def _out_ffn_body(x_ref, o_ref, s_ref, cprev_ref, wo_ref, g2_ref, wup_ref, cw_ref, cb_ref, wdn_ref, gf_ref,
                  y_ref, cout_ref, acc_ref, *, rows_are_positions):
    tm = x_ref.shape[0]
    x1 = (x_ref[...]
          + jnp.dot(o_ref[...], wo_ref[:D_ATT, :], preferred_element_type=F32)
          + jnp.dot(s_ref[...], wo_ref[D_ATT:, :], preferred_element_type=F32))
    xn = _rms(x1, g2_ref[...]).astype(BF16)

    if rows_are_positions:
        @pl.when(pl.program_id(1) == 0)
        def _():
            cout_ref[...] = cprev_ref[...]
        hist = cout_ref[...]
        rid = lax.broadcasted_iota(jnp.int32, (tm, FF_CHUNK), 0)
    else:
        hist = cprev_ref[...]

    acc_ref[...] = jnp.zeros(acc_ref.shape, F32)
    for c in range(N_FF_CHUNKS):
        cs = slice(c * FF_CHUNK, (c + 1) * FF_CHUNK)
        cs_up = slice(D_FF + c * FF_CHUNK, D_FF + (c + 1) * FF_CHUNK)
        gate = jnp.dot(xn, wup_ref[:, cs], preferred_element_type=F32)
        up = jnp.dot(xn, wup_ref[:, cs_up], preferred_element_type=F32)
        if rows_are_positions:
            p0, p1 = hist[0:1, cs], hist[1:2, cs]
            g1 = jnp.where(rid == 0, p1, pltpu.roll(gate, 1, 0))
            g2 = jnp.where(rid == 0, p0, jnp.where(rid == 1, p1, pltpu.roll(gate, 2, 0)))
            cout_ref[:, cs] = gate[tm - 2:, :]
        else:
            g2, g1 = hist[:, cs], hist[:, cs_up]
            cout_ref[:, cs] = g1
            cout_ref[:, cs_up] = gate
        gc = cb_ref[:, cs] + cw_ref[0:1, cs] * g2 + cw_ref[1:2, cs] * g1 + cw_ref[2:3, cs] * gate
        act = (gc * _sigmoid(gc) * up).astype(BF16)
        acc_ref[...] += jnp.dot(act, wdn_ref[cs, :], preferred_element_type=F32)
    y_ref[...] = _rms(x1 + acc_ref[...], gf_ref[...])


def _out_ffn(x, o, s, conv_prev, w_out_bf, g2, w_up_bf, conv_w, conv_b, w_down_bf, gf, *, tm, rows_are_positions):
    b, n, _ = x.shape
    conv_spec = pl.BlockSpec((None,) + conv_prev.shape[1:], lambda i, j: (i, 0, 0))
    body = functools.partial(_out_ffn_body, rows_are_positions=rows_are_positions)
    return pl.pallas_call(
        body,
        grid=(b, n // tm),
        in_specs=[pl.BlockSpec((None, tm, D_MODEL), lambda i, j: (i, j, 0)),
                  pl.BlockSpec((None, tm, D_ATT), lambda i, j: (i, j, 0)),
                  pl.BlockSpec((tm, D_SSM), lambda i, j: (j, i)),
                  conv_spec,
                  _resident((D_MODEL, D_MODEL)), _resident((1, D_MODEL)),
                  _resident((D_MODEL, 2 * D_FF)), _resident((3, D_FF)), _resident((1, D_FF)),
                  _resident((D_FF, D_MODEL)), _resident((1, D_MODEL))],
        out_specs=[pl.BlockSpec((None, tm, D_MODEL), lambda i, j: (i, j, 0)), conv_spec],
        out_shape=[jax.ShapeDtypeStruct(x.shape, F32), jax.ShapeDtypeStruct(conv_prev.shape, F32)],
        scratch_shapes=[pltpu.VMEM((tm, D_MODEL), F32)],
        compiler_params=_params(2), name="out_ffn")(
            x, o, s, conv_prev, w_out_bf, g2, w_up_bf, conv_w, conv_b, w_down_bf, gf)


def kernel(x_prompt, x_sample, cache_k, cache_v, state_ssm_re, state_ssm_im, cache_ffn_conv, page_table,
           norm1_g, w_in, lam_q1, lam_k1, lam_q2, lam_k2, subln_g, ssm_lam_re, ssm_lam_im, ssm_log_dt,
           ssm_b_re, ssm_b_im, ssm_c_re, ssm_c_im, ssm_d, w_glu, b_glu, w_out, norm2_g, w_up, conv_w,
           conv_b, w_down, final_g):
    assert w_in.shape[0] == 1, "single-layer step"
    n_batch, seq, _ = x_prompt.shape
    n_dec = x_sample.shape[0]
    n_pool = cache_k.shape[1]

    w_in_bf, w_out_bf = w_in[0].astype(BF16), w_out[0].astype(BF16)
    w_up_bf, w_down_bf, w_glu_bf = w_up[0].astype(BF16), w_down[0].astype(BF16), w_glu[0].astype(BF16)
    g1, g2, gf = norm1_g[0].reshape(1, -1), norm2_g[0].reshape(1, -1), final_g.reshape(1, -1)
    lamp = jnp.stack([lam_q1[0], lam_k1[0], lam_q2[0], lam_k2[0]])
    sg = subln_g[0].reshape(1, V_DIM)
    a_re, a_im, bb_re, bb_im = _ssm_discretize(ssm_lam_re[0], ssm_lam_im[0], ssm_log_dt[0],
                                               ssm_b_re[0], ssm_b_im[0])
    ssm_w = (a_re, a_im, _blockdiag_in(bb_re).astype(BF16), _blockdiag_in(bb_im).astype(BF16),
             _blockdiag_out(ssm_c_re[0]).astype(BF16), _blockdiag_out(ssm_c_im[0]).astype(BF16),
             ssm_d[0].reshape(1, -1), w_glu_bf, b_glu[0].reshape(1, -1))
    ffn_w = (w_out_bf, g2, w_up_bf, conv_w[0], conv_b[0].reshape(1, -1), w_down_bf, gf)

    q, k, v, u, k_bf, v_bf = _in_proj(x_prompt, g1, w_in_bf, tm=512, q_dtype=BF16, bf16_kv=True)
    o = _prompt_attention(lamp, sg, q, k_bf, v_bf, tq=256)
    s, h_p = _ssm_glu(u.reshape(seq * n_batch, D_SSM), jnp.zeros((n_batch, 2 * N_STATE), F32), *ssm_w,
                      rows_per_step=n_batch, steps_per_block=64)
    y_p, conv_p = _out_ffn(x_prompt, o, s.reshape(seq, n_batch * D_SSM),
                           jnp.zeros((n_batch, 2, D_FF), F32), *ffn_w, tm=512, rows_are_positions=True)

    xs = x_sample.reshape(1, n_dec, D_MODEL)
    qs, ks, vs, us = _in_proj(xs, g1, w_in_bf, tm=n_dec, q_dtype=F32, bf16_kv=False)
    row3 = lambda t: t.reshape(n_dec, 1, D_QK)
    o_s = _decode_attention(page_table, lamp, sg, row3(qs), row3(ks), row3(vs),
                            cache_k.reshape(n_pool, PAGE, D_QK), cache_v.reshape(n_pool, PAGE, D_ATT),
                            pages_per_chunk=16)
    h0_s = jnp.concatenate([state_ssm_re[0].reshape(n_dec, N_STATE),
                            state_ssm_im[0].reshape(n_dec, N_STATE)], axis=-1)
    s_s, h_s = _ssm_glu(us, h0_s, *ssm_w, rows_per_step=n_dec, steps_per_block=1)
    y_s, conv_s = _out_ffn(xs, o_s.reshape(1, n_dec, D_ATT).astype(BF16), s_s,
                           cache_ffn_conv[0].reshape(1, n_dec, 2 * D_FF), *ffn_w,
                           tm=n_dec, rows_are_positions=False)

    heads_k = lambda t, lead: t.reshape(1, *lead, N_HEADS, 2, HEAD_DIM)
    heads_v = lambda t, lead: t.reshape(1, *lead, N_HEADS, V_DIM)
    state = lambda h, n: h.reshape(1, n, N_GROUPS, SSM_STATE)
    return (y_p, y_s.reshape(n_dec, 1, D_MODEL),
            heads_k(k, (n_batch, seq)), heads_v(v, (n_batch, seq)),
            heads_k(ks, (n_dec, 1)), heads_v(vs, (n_dec, 1)),
            state(h_p[:, :N_STATE], n_batch), state(h_p[:, N_STATE:], n_batch),
            state(h_s[:, :N_STATE], n_dec), state(h_s[:, N_STATE:], n_dec),
            conv_p.reshape(1, n_batch, 2, D_FF), conv_s.reshape(1, n_dec, 2, D_FF))
```

```python
import functools
import math

import jax
import jax.numpy as jnp
from jax import lax
from jax.experimental import pallas as pl
from jax.experimental.pallas import tpu as pltpu

F32 = jnp.float32
BF16 = jnp.bfloat16

D_MODEL = 1024
N_HEADS = 4
HEAD_DIM = 64
V_DIM = 2 * HEAD_DIM
D_ATT = N_HEADS * V_DIM
D_QK = N_HEADS * 2 * HEAD_DIM
D_SSM = 512
SSM_GROUP = 16
N_GROUPS = D_SSM // SSM_GROUP
SSM_STATE = 64
N_STATE = N_GROUPS * SSM_STATE
D_IN = 2 * D_QK + D_ATT + D_SSM
D_FF = 2816
PAGE = 128
EPS = 1e-6
LAM_INIT = 0.8 - 0.6 * math.exp(-0.3 * 0)
QK_SCALE = HEAD_DIM ** -0.5

LANES = 128
SUBLANES = 8
MXU_COLS = 256
VMEM_LIMIT = 56 * 1024 * 1024

FF_CHUNK = MXU_COLS
N_FF_CHUNKS = D_FF // FF_CHUNK
SSM_CHUNK_GROUPS = MXU_COLS // SSM_GROUP
SSM_CHUNK_STATES = SSM_CHUNK_GROUPS * SSM_STATE
N_SSM_CHUNKS = D_SSM // MXU_COLS
SCAN_COLS = 512


def _params(n_axes, vmem=VMEM_LIMIT):
    return pltpu.CompilerParams(dimension_semantics=("arbitrary",) * n_axes, vmem_limit_bytes=vmem)


def _rms(x, g):
    return x * lax.rsqrt(jnp.mean(x * x, axis=-1, keepdims=True) + EPS) * g


def _sigmoid(x):
    return 1.0 / (1.0 + jnp.exp(-x))


def _diff_lambda(lamp_ref):
    lp = lamp_ref[...]
    a1 = jnp.sum(lp[0:1] * lp[1:2], axis=-1, keepdims=True)
    a2 = jnp.sum(lp[2:3] * lp[3:4], axis=-1, keepdims=True)
    return jnp.exp(a1) - jnp.exp(a2) + LAM_INIT


def _resident(shape):
    return pl.BlockSpec(shape, lambda *_: (0,) * len(shape), pipeline_mode=pl.Buffered(1))


def _disc_body(lr_ref, li_ref, ldt_ref, br_ref, bi_ref, ar_ref, ai_ref, bbr_ref, bbi_ref):
    lr, li = lr_ref[...], li_ref[...]
    dt = jnp.exp(ldt_ref[...])
    mag = jnp.exp(lr * dt)
    a_re, a_im = mag * jnp.cos(li * dt), mag * jnp.sin(li * dt)
    den = lr * lr + li * li
    nr, ni = a_re - 1.0, a_im
    coef_re = (nr * lr + ni * li) / den
    coef_im = (ni * lr - nr * li) / den
    br, bi = br_ref[...], bi_ref[...]
    bbr_ref[...] = coef_re * br - coef_im * bi
    bbi_ref[...] = coef_re * bi + coef_im * br
    ar_ref[...] = a_re
    ai_ref[...] = a_im


def _ssm_discretize(lam_re, lam_im, log_dt, b_re, b_im):
    row = jax.ShapeDtypeStruct((1, N_STATE), F32)
    mat = jax.ShapeDtypeStruct((SSM_GROUP, N_STATE), F32)
    return pl.pallas_call(_disc_body, out_shape=(row, row, mat, mat), name="ssm_discretize")(
        lam_re.reshape(1, N_STATE), lam_im.reshape(1, N_STATE),
        jnp.repeat(log_dt, SSM_STATE).reshape(1, N_STATE),
        b_re.transpose(2, 0, 1).reshape(SSM_GROUP, N_STATE),
        b_im.transpose(2, 0, 1).reshape(SSM_GROUP, N_STATE))


def _blockdiag_in(bbt):
    t = bbt.reshape(SSM_GROUP, N_SSM_CHUNKS, SSM_CHUNK_GROUPS, SSM_STATE)
    eye = jnp.eye(SSM_CHUNK_GROUPS, dtype=bbt.dtype)
    return jnp.einsum('cngp,gh->ngchp', t, eye).reshape(N_SSM_CHUNKS, MXU_COLS, SSM_CHUNK_STATES)


def _blockdiag_out(c):
    t = c.reshape(N_SSM_CHUNKS, SSM_CHUNK_GROUPS, SSM_GROUP, SSM_STATE)
    eye = jnp.eye(SSM_CHUNK_GROUPS, dtype=c.dtype)
    return jnp.einsum('ngkp,gh->ngphk', t, eye).reshape(N_SSM_CHUNKS, SSM_CHUNK_STATES, MXU_COLS)


def _inproj_body(x_ref, g_ref, w_ref, q_ref, k_ref, v_ref, u_ref, *maybe_bf16_kv):
    xn = _rms(x_ref[...], g_ref[...]).astype(BF16)

    def proj(lo):
        return jnp.dot(xn, w_ref[:, lo:lo + D_QK], preferred_element_type=F32)

    q_ref[...] = (proj(0) * QK_SCALE).astype(q_ref.dtype)
    k = proj(D_QK)
    v = proj(2 * D_QK)
    k_ref[...] = k
    v_ref[...] = v
    u_ref[...] = proj(2 * D_QK + D_ATT)
    if maybe_bf16_kv:
        kb_ref, vb_ref = maybe_bf16_kv
        kb_ref[...] = k.astype(BF16)
        vb_ref[...] = v.astype(BF16)


def _in_proj(x, g, w_bf, *, tm, q_dtype, bf16_kv):
    b, s, _ = x.shape
    tok = lambda dt: jax.ShapeDtypeStruct((b, s, D_QK), dt)
    tok_spec = pl.BlockSpec((None, tm, D_QK), lambda i, j: (i, j, 0))
    out_shape = [tok(q_dtype), tok(F32), tok(F32), jax.ShapeDtypeStruct((s, b * D_SSM), F32)]
    out_specs = [tok_spec, tok_spec, tok_spec, pl.BlockSpec((tm, D_SSM), lambda i, j: (j, i))]
    if bf16_kv:
        out_shape += [tok(BF16), tok(BF16)]
        out_specs += [tok_spec, tok_spec]
    return pl.pallas_call(
        _inproj_body,
        grid=(b, s // tm),
        in_specs=[pl.BlockSpec((None, tm, D_MODEL), lambda i, j: (i, j, 0)),
                  _resident((1, D_MODEL)), _resident((D_MODEL, D_IN))],
        out_specs=out_specs, out_shape=out_shape,
        compiler_params=_params(2), name="in_proj")(x, g, w_bf)


POS_RADIX = 256


def _alibi_key_features(seq):
    j = lax.broadcasted_iota(jnp.int32, (seq, LANES), 0)
    lane = lax.broadcasted_iota(jnp.int32, (seq, LANES), 1)
    feat = jnp.where(lane == 0, j // POS_RADIX, jnp.where(lane == 1, j % POS_RADIX, 0))
    return feat.astype(BF16)


def _prompt_attn_body(lamp_ref, sg_ref, q_ref, k_ref, v_ref, pos_ref, o_ref, m_scr, acc_scr, qs_scr, *, tq, tk):
    qi = pl.program_id(1)
    lam = _diff_lambda(lamp_ref)
    lane = lax.broadcasted_iota(jnp.int32, (1, LANES), 1)
    n_full = (qi * tq) // tk
    diag_off = qi * tq - n_full * tk
    row_in_tile = lax.broadcasted_iota(jnp.int32, (2 * tq, tk), 0) % tq
    col = lax.broadcasted_iota(jnp.int32, (2 * tq, tk), 1)
    causal = row_in_tile + diag_off >= col
    ones_col = jnp.broadcast_to(jnp.where(lane == 0, 1.0, 0.0).astype(BF16), (tk, LANES))
    nt = (((1,), (1,)), ((), ()))

    heads = [slice(h * V_DIM, (h + 1) * V_DIM) for h in range(N_HEADS)]
    for h, hs in enumerate(heads):
        slope = 2.0 ** (-8.0 * (h + 1) / N_HEADS)
        q = q_ref[:, hs]
        zero = jnp.zeros_like(q)
        q_pos = jnp.broadcast_to(
            jnp.where(lane == 0, POS_RADIX * slope, jnp.where(lane == 1, slope, 0.0)).astype(BF16), (tq, LANES))
        qs_scr[h, :tq, :LANES] = jnp.where(lane < HEAD_DIM, q, zero)
        qs_scr[h, tq:, :LANES] = jnp.where(lane >= HEAD_DIM, q, zero)
        qs_scr[h, :tq, LANES:] = q_pos
        qs_scr[h, tq:, LANES:] = q_pos
    m_scr[...] = jnp.full(m_scr.shape, -jnp.inf, F32)
    acc_scr[...] = jnp.zeros(acc_scr.shape, F32)

    def chunk(j, masked):
        rows = pl.ds(pl.multiple_of(j * tk, tk), tk)
        pos = pos_ref[rows, :]
        for h, hs in enumerate(heads):
            kx = jnp.concatenate([k_ref[rows, hs], pos], axis=1)
            vx = jnp.concatenate([v_ref[rows, hs], ones_col], axis=1)
            s = lax.dot_general(qs_scr[h], kx, nt, preferred_element_type=F32)
            if masked:
                s = jnp.where(causal, s, -jnp.inf)
            m_old = m_scr[h]
            m_new = jnp.maximum(m_old, jnp.max(s, axis=-1, keepdims=True))
            p = jnp.exp(s - m_new).astype(BF16)
            acc_scr[h] = (jnp.exp(m_old - m_new) * acc_scr[h]
                          + jnp.dot(p, vx, preferred_element_type=F32))
            m_scr[h] = m_new

    def unmasked(j, carry):
        chunk(j, False)
        return carry

    lax.fori_loop(0, n_full, unmasked, 0)
    chunk(n_full, True)
    for h, hs in enumerate(heads):
        acc = acc_scr[h]
        a1, a2 = acc[:tq], acc[tq:]
        o = (a1[:, :V_DIM] / a1[:, V_DIM:V_DIM + 1]
             - lam * (a2[:, :V_DIM] / a2[:, V_DIM:V_DIM + 1]))
        o_ref[:, hs] = (_rms(o, sg_ref[...]) * (1.0 - LAM_INIT)).astype(o_ref.dtype)


def _prompt_attention(lamp, subln_g, q, k, v, *, tq, tk):
    b, s, _ = q.shape
    assert s % tk == 0 and tk % tq == 0
    seq_spec = pl.BlockSpec((None, s, D_QK), lambda i, j: (i, 0, 0))
    return pl.pallas_call(
        functools.partial(_prompt_attn_body, tq=tq, tk=tk),
        grid=(b, s // tq),
        in_specs=[_resident((4, HEAD_DIM)), _resident((1, V_DIM)),
                  pl.BlockSpec((None, tq, D_QK), lambda i, j: (i, j, 0)), seq_spec, seq_spec,
                  _resident((s, LANES))],
        out_specs=pl.BlockSpec((None, tq, D_ATT), lambda i, j: (i, j, 0)),
        out_shape=jax.ShapeDtypeStruct((b, s, D_ATT), BF16),
        scratch_shapes=[pltpu.VMEM((N_HEADS, 2 * tq, 1), F32), pltpu.VMEM((N_HEADS, 2 * tq, 2 * LANES), F32),
                        pltpu.VMEM((N_HEADS, 2 * tq, 2 * LANES), BF16)],
        compiler_params=_params(2), name="prompt_attn")(lamp, subln_g, q, k, v, _alibi_key_features(s))


def _decode_attn_body(pt_ref, lamp_ref, sg_ref, q_ref, kn_ref, vn_ref, ck_hbm, cv_hbm, o_ref,
                      kbuf, vbuf, sem, m_ref, l_ref, acc_ref, *, n_chunks, pages_per_chunk, n_pages):
    b, c = pl.program_id(0), pl.program_id(1)
    step = b * n_chunks + c
    n_steps = pl.num_programs(0) * n_chunks
    slot = step % 2
    chunk_tokens = pages_per_chunk * PAGE
    past_len = n_pages * PAGE

    def page_copies(seq, chunk, slot_):
        copies = []
        for p in range(pages_per_chunk):
            page = pt_ref[seq * n_pages + chunk * pages_per_chunk + p]
            copies.append(pltpu.make_async_copy(
                ck_hbm.at[page], kbuf.at[slot_, :, pl.ds(p * PAGE, PAGE)], sem.at[0, slot_]))
            copies.append(pltpu.make_async_copy(
                cv_hbm.at[page], vbuf.at[slot_, pl.ds(p * PAGE * N_HEADS, PAGE * N_HEADS), :], sem.at[1, slot_]))
        return copies

    @pl.when(step == 0)
    def _():
        for cp in page_copies(0, 0, 0):
            cp.start()

    @pl.when(step + 1 < n_steps)
    def _():
        nxt = step + 1
        for cp in page_copies(nxt // n_chunks, nxt % n_chunks, 1 - slot):
            cp.start()

    @pl.when(c == 0)
    def _():
        m_ref[...] = jnp.full(m_ref.shape, -jnp.inf, F32)
        l_ref[...] = jnp.zeros(l_ref.shape, F32)
        acc_ref[...] = jnp.zeros(acc_ref.shape, F32)

    n_maps = 2 * N_HEADS
    row = lax.broadcasted_iota(jnp.int32, (n_maps, D_QK), 0)
    lane = lax.broadcasted_iota(jnp.int32, (n_maps, D_QK), 1)
    qblk = jnp.where(lane // HEAD_DIM == row, jnp.broadcast_to(q_ref[...], (n_maps, D_QK)), 0.0)
    row1 = lax.broadcasted_iota(jnp.int32, (n_maps, 1), 0)
    slope = jnp.exp2(-8.0 * ((row1 // 2) + 1).astype(F32) / N_HEADS)

    for cp in page_copies(b, c, slot):
        cp.wait()

    kt = kbuf[slot].astype(BF16)
    s = jnp.dot(qblk.astype(BF16), kt, preferred_element_type=F32)
    pos = c * chunk_tokens + lax.broadcasted_iota(jnp.int32, (1, chunk_tokens), 1)
    s = s - slope * (past_len - pos).astype(F32)
    m_old = m_ref[...]
    m_new = jnp.maximum(m_old, jnp.max(s, axis=-1, keepdims=True))
    alpha = jnp.exp(m_old - m_new)
    p = jnp.exp(s - m_new)
    l_ref[...] = alpha * l_ref[...] + jnp.sum(p, axis=-1, keepdims=True)
    pb = p.astype(BF16)
    for h in range(N_HEADS):
        hs = slice(h * V_DIM, (h + 1) * V_DIM)
        vh = vbuf[slot, pl.ds(h, chunk_tokens, stride=N_HEADS), :].astype(BF16)
        acc_ref[:, hs] = alpha * acc_ref[:, hs] + jnp.dot(pb, vh, preferred_element_type=F32)
    m_ref[...] = m_new

    @pl.when(c == n_chunks - 1)
    def _():
        s_new = jnp.sum(qblk * kn_ref[...], axis=-1, keepdims=True)
        m_old = m_ref[...]
        m_fin = jnp.maximum(m_old, s_new)
        alpha = jnp.exp(m_old - m_fin)
        p_new = jnp.exp(s_new - m_fin)
        l_fin = alpha * l_ref[...] + p_new
        acc = alpha * acc_ref[...] + p_new * vn_ref[...]
        lam = _diff_lambda(lamp_ref)
        coef = jnp.where(row1 % 2 == 0, 1.0, -lam) / l_fin
        own = lane // V_DIM == row // 2
        o = jnp.sum(jnp.where(own, coef * acc, 0.0), axis=0, keepdims=True)
        for h in range(N_HEADS):
            hs = slice(h * V_DIM, (h + 1) * V_DIM)
            o_ref[:, hs] = _rms(o[:, hs], sg_ref[...]) * (1.0 - LAM_INIT)


def _decode_attention(page_table, lamp, subln_g, q, k_new, v_new, cache_k, cache_v, *, pages_per_chunk):
    n, n_pages = page_table.shape
    n_chunks = n_pages // pages_per_chunk
    chunk_tokens = pages_per_chunk * PAGE
    row_spec = pl.BlockSpec((None, 1, D_QK), lambda i, j, pt: (i, 0, 0))
    body = functools.partial(_decode_attn_body, n_chunks=n_chunks, pages_per_chunk=pages_per_chunk,
                             n_pages=n_pages)
    grid_spec = pltpu.PrefetchScalarGridSpec(
        num_scalar_prefetch=1,
        grid=(n, n_chunks),
        in_specs=[pl.BlockSpec((4, HEAD_DIM), lambda i, j, pt: (0, 0)),
                  pl.BlockSpec((1, V_DIM), lambda i, j, pt: (0, 0)),
                  row_spec, row_spec, row_spec,
                  pl.BlockSpec(memory_space=pl.ANY), pl.BlockSpec(memory_space=pl.ANY)],
        out_specs=row_spec,
        scratch_shapes=[pltpu.VMEM((2, D_QK, chunk_tokens), F32),
                        pltpu.VMEM((2, chunk_tokens * N_HEADS, V_DIM), F32),
                        pltpu.SemaphoreType.DMA((2, 2)),
                        pltpu.VMEM((2 * N_HEADS, 1), F32),
                        pltpu.VMEM((2 * N_HEADS, 1), F32),
                        pltpu.VMEM((2 * N_HEADS, D_ATT), F32)])
    return pl.pallas_call(
        body, grid_spec=grid_spec, out_shape=jax.ShapeDtypeStruct((n, 1, D_ATT), F32),
        compiler_params=_params(2), name="decode_attn")(
            page_table.reshape(-1), lamp, subln_g, q, k_new, v_new, cache_k, cache_v)


def _ssm_body(u_ref, h0_ref, ar_ref, ai_ref, bre_ref, bim_ref, cre_ref, cim_ref, d_ref, wg_ref, bg_ref,
              s_ref, h_ref, x_scr, *, rows_per_step, n_steps):
    r = rows_per_step

    @pl.when(pl.program_id(0) == 0)
    def _():
        h_ref[...] = h0_ref[...]

    u = u_ref[...]
    ub = u.astype(BF16)
    for n in range(N_SSM_CHUNKS):
        un = ub[:, n * MXU_COLS:(n + 1) * MXU_COLS]
        cols = slice(n * SSM_CHUNK_STATES, (n + 1) * SSM_CHUNK_STATES)
        x_scr[:, cols] = jnp.dot(un, bre_ref[n], preferred_element_type=F32)
        x_scr[:, N_STATE + n * SSM_CHUNK_STATES:N_STATE + (n + 1) * SSM_CHUNK_STATES] = jnp.dot(
            un, bim_ref[n], preferred_element_type=F32)

    for cb in range(N_STATE // SCAN_COLS):
        re = slice(cb * SCAN_COLS, (cb + 1) * SCAN_COLS)
        im = slice(N_STATE + cb * SCAN_COLS, N_STATE + (cb + 1) * SCAN_COLS)
        a_r = jnp.broadcast_to(ar_ref[:, re], (r, SCAN_COLS))
        a_i = jnp.broadcast_to(ai_ref[:, re], (r, SCAN_COLS))

        def step(t, carry, re=re, im=im, a_r=a_r, a_i=a_i):
            hr, hi = carry
            rows = pl.ds(pl.multiple_of(t * r, r), r)
            nhr = a_r * hr - a_i * hi + x_scr[rows, re]
            nhi = a_r * hi + a_i * hr + x_scr[rows, im]
            x_scr[rows, re] = nhr
            x_scr[rows, im] = nhi
            return nhr, nhi

        carry = (h_ref[:, re], h_ref[:, im])
        if n_steps == 1:
            hr, hi = step(0, carry)
        else:
            hr, hi = lax.fori_loop(0, n_steps, step, carry, unroll=8)
        h_ref[:, re] = hr
        h_ref[:, im] = hi

    ys = []
    for n in range(N_SSM_CHUNKS):
        cols = slice(n * SSM_CHUNK_STATES, (n + 1) * SSM_CHUNK_STATES)
        cols_im = slice(N_STATE + n * SSM_CHUNK_STATES, N_STATE + (n + 1) * SSM_CHUNK_STATES)
        ys.append(jnp.dot(x_scr[:, cols].astype(BF16), cre_ref[n], preferred_element_type=F32)
                  - jnp.dot(x_scr[:, cols_im].astype(BF16), cim_ref[n], preferred_element_type=F32))
    y = jnp.concatenate(ys, axis=-1) + d_ref[...] * u
    g = 0.5 * y * (1.0 + jnp.tanh(math.sqrt(2.0 / math.pi) * (y + 0.044715 * (y * y * y))))
    z = jnp.dot(g.astype(BF16), wg_ref[...], preferred_element_type=F32) + bg_ref[...]
    s_ref[...] = (g * _sigmoid(z)).astype(s_ref.dtype)


def _ssm_glu(u, h0, a_re, a_im, b_re_bd, b_im_bd, c_re_bd, c_im_bd, d_skip, w_glu_bf, b_glu,
             *, rows_per_step, steps_per_block):
    n_rows = u.shape[0]
    blk = rows_per_step * steps_per_block
    body = functools.partial(_ssm_body, rows_per_step=rows_per_step, n_steps=steps_per_block)
    return pl.pallas_call(
        body,
        grid=(n_rows // blk,),
        in_specs=[pl.BlockSpec((blk, D_SSM), lambda i: (i, 0)),
                  _resident((rows_per_step, 2 * N_STATE)),
                  _resident((1, N_STATE)), _resident((1, N_STATE)),
                  _resident((N_SSM_CHUNKS, MXU_COLS, SSM_CHUNK_STATES)),
                  _resident((N_SSM_CHUNKS, MXU_COLS, SSM_CHUNK_STATES)),
                  _resident((N_SSM_CHUNKS, SSM_CHUNK_STATES, MXU_COLS)),
                  _resident((N_SSM_CHUNKS, SSM_CHUNK_STATES, MXU_COLS)),
                  _resident((1, D_SSM)), _resident((D_SSM, D_SSM)), _resident((1, D_SSM))],
        out_specs=[pl.BlockSpec((blk, D_SSM), lambda i: (i, 0)),
                   pl.BlockSpec((rows_per_step, 2 * N_STATE), lambda i: (0, 0))],
        out_shape=[jax.ShapeDtypeStruct((n_rows, D_SSM), BF16),
                   jax.ShapeDtypeStruct((rows_per_step, 2 * N_STATE), F32)],
        scratch_shapes=[pltpu.VMEM((blk, 2 * N_STATE), F32)],
        compiler_params=_params(1), name="ssm_glu")(
            u, h0, a_re, a_im, b_re_bd, b_im_bd, c_re_bd, c_im_bd, d_skip, w_glu_bf, b_glu)


def _out_ffn_body(x_ref, o_ref, s_ref, cprev_ref, wo_ref, g2_ref, wup_ref, cw_ref, cb_ref, wdn_ref, gf_ref,
                  y_ref, cout_ref, acc_ref, *, rows_are_positions):
    tm = x_ref.shape[0]
    x1 = (x_ref[...]
          + jnp.dot(o_ref[...], wo_ref[:D_ATT, :], preferred_element_type=F32)
          + jnp.dot(s_ref[...], wo_ref[D_ATT:, :], preferred_element_type=F32))
    xn = _rms(x1, g2_ref[...]).astype(BF16)

    if rows_are_positions:
        @pl.when(pl.program_id(1) == 0)
        def _():
            cout_ref[...] = cprev_ref[...]
        hist = cout_ref[...]
        rid = lax.broadcasted_iota(jnp.int32, (tm, FF_CHUNK), 0)
    else:
        hist = cprev_ref[...]

    acc_ref[...] = jnp.zeros(acc_ref.shape, F32)
    for c in range(N_FF_CHUNKS):
        cs = slice(c * FF_CHUNK, (c + 1) * FF_CHUNK)
        cs_up = slice(D_FF + c * FF_CHUNK, D_FF + (c + 1) * FF_CHUNK)
        gate = jnp.dot(xn, wup_ref[:, cs], preferred_element_type=F32)
        up = jnp.dot(xn, wup_ref[:, cs_up], preferred_element_type=F32)
        if rows_are_positions:
            p0, p1 = hist[0:1, cs], hist[1:2, cs]
            g1 = jnp.where(rid == 0, p1, pltpu.roll(gate, 1, 0))
            g2 = jnp.where(rid == 0, p0, jnp.where(rid == 1, p1, pltpu.roll(gate, 2, 0)))
            cout_ref[:, cs] = gate[tm - 2:, :]
        else:
            g2, g1 = hist[:, cs], hist[:, cs_up]
            cout_ref[:, cs] = g1
            cout_ref[:, cs_up] = gate
        gc = cb_ref[:, cs] + cw_ref[0:1, cs] * g2 + cw_ref[1:2, cs] * g1 + cw_ref[2:3, cs] * gate
        act = (gc * _sigmoid(gc) * up).astype(BF16)
        acc_ref[...] += jnp.dot(act, wdn_ref[cs, :], preferred_element_type=F32)
    y_ref[...] = _rms(x1 + acc_ref[...], gf_ref[...])


def _out_ffn(x, o, s, conv_prev, w_out_bf, g2, w_up_bf, conv_w, conv_b, w_down_bf, gf, *, tm, rows_are_positions):
    b, n, _ = x.shape
    conv_spec = pl.BlockSpec((None,) + conv_prev.shape[1:], lambda i, j: (i, 0, 0))
    body = functools.partial(_out_ffn_body, rows_are_positions=rows_are_positions)
    return pl.pallas_call(
        body,
        grid=(b, n // tm),
        in_specs=[pl.BlockSpec((None, tm, D_MODEL), lambda i, j: (i, j, 0)),
                  pl.BlockSpec((None, tm, D_ATT), lambda i, j: (i, j, 0)),
                  pl.BlockSpec((tm, D_SSM), lambda i, j: (j, i)),
                  conv_spec,
                  _resident((D_MODEL, D_MODEL)), _resident((1, D_MODEL)),
                  _resident((D_MODEL, 2 * D_FF)), _resident((3, D_FF)), _resident((1, D_FF)),
                  _resident((D_FF, D_MODEL)), _resident((1, D_MODEL))],
        out_specs=[pl.BlockSpec((None, tm, D_MODEL), lambda i, j: (i, j, 0)), conv_spec],
        out_shape=[jax.ShapeDtypeStruct(x.shape, F32), jax.ShapeDtypeStruct(conv_prev.shape, F32)],
        scratch_shapes=[pltpu.VMEM((tm, D_MODEL), F32)],
        compiler_params=_params(2), name="out_ffn")(
            x, o, s, conv_prev, w_out_bf, g2, w_up_bf, conv_w, conv_b, w_down_bf, gf)


def kernel(x_prompt, x_sample, cache_k, cache_v, state_ssm_re, state_ssm_im, cache_ffn_conv, page_table,
           norm1_g, w_in, lam_q1, lam_k1, lam_q2, lam_k2, subln_g, ssm_lam_re, ssm_lam_im, ssm_log_dt,
           ssm_b_re, ssm_b_im, ssm_c_re, ssm_c_im, ssm_d, w_glu, b_glu, w_out, norm2_g, w_up, conv_w,
           conv_b, w_down, final_g):
    assert w_in.shape[0] == 1, "single-layer step"
    n_batch, seq, _ = x_prompt.shape
    n_dec = x_sample.shape[0]
    n_pool = cache_k.shape[1]

    w_in_bf, w_out_bf = w_in[0].astype(BF16), w_out[0].astype(BF16)
    w_up_bf, w_down_bf, w_glu_bf = w_up[0].astype(BF16), w_down[0].astype(BF16), w_glu[0].astype(BF16)
    g1, g2, gf = norm1_g[0].reshape(1, -1), norm2_g[0].reshape(1, -1), final_g.reshape(1, -1)
    lamp = jnp.stack([lam_q1[0], lam_k1[0], lam_q2[0], lam_k2[0]])
    sg = subln_g[0].reshape(1, V_DIM)
    a_re, a_im, bb_re, bb_im = _ssm_discretize(ssm_lam_re[0], ssm_lam_im[0], ssm_log_dt[0],
                                               ssm_b_re[0], ssm_b_im[0])
    ssm_w = (a_re, a_im, _blockdiag_in(bb_re).astype(BF16), _blockdiag_in(bb_im).astype(BF16),
             _blockdiag_out(ssm_c_re[0]).astype(BF16), _blockdiag_out(ssm_c_im[0]).astype(BF16),
             ssm_d[0].reshape(1, -1), w_glu_bf, b_glu[0].reshape(1, -1))
    ffn_w = (w_out_bf, g2, w_up_bf, conv_w[0], conv_b[0].reshape(1, -1), w_down_bf, gf)

    q, k, v, u, k_bf, v_bf = _in_proj(x_prompt, g1, w_in_bf, tm=512, q_dtype=BF16, bf16_kv=True)
    o = _prompt_attention(lamp, sg, q, k_bf, v_bf, tq=256, tk=512)
    s, h_p = _ssm_glu(u.reshape(seq * n_batch, D_SSM), jnp.zeros((n_batch, 2 * N_STATE), F32), *ssm_w,
                      rows_per_step=n_batch, steps_per_block=64)
    y_p, conv_p = _out_ffn(x_prompt, o, s.reshape(seq, n_batch * D_SSM),
                           jnp.zeros((n_batch, 2, D_FF), F32), *ffn_w, tm=512, rows_are_positions=True)

    xs = x_sample.reshape(1, n_dec, D_MODEL)
    qs, ks, vs, us = _in_proj(xs, g1, w_in_bf, tm=n_dec, q_dtype=F32, bf16_kv=False)
    row3 = lambda t: t.reshape(n_dec, 1, D_QK)
    o_s = _decode_attention(page_table, lamp, sg, row3(qs), row3(ks), row3(vs),
                            cache_k.transpose(0, 1, 3, 4, 5, 2).reshape(n_pool, D_QK, PAGE),
                            cache_v.reshape(n_pool, PAGE * N_HEADS, V_DIM),
                            pages_per_chunk=16)
    h0_s = jnp.concatenate([state_ssm_re[0].reshape(n_dec, N_STATE),
                            state_ssm_im[0].reshape(n_dec, N_STATE)], axis=-1)
    s_s, h_s = _ssm_glu(us, h0_s, *ssm_w, rows_per_step=n_dec, steps_per_block=1)
    y_s, conv_s = _out_ffn(xs, o_s.reshape(1, n_dec, D_ATT).astype(BF16), s_s,
                           cache_ffn_conv[0].reshape(1, n_dec, 2 * D_FF), *ffn_w,
                           tm=n_dec, rows_are_positions=False)

    heads_k = lambda t, lead: t.reshape(1, *lead, N_HEADS, 2, HEAD_DIM)
    heads_v = lambda t, lead: t.reshape(1, *lead, N_HEADS, V_DIM)
    state = lambda h, n: h.reshape(1, n, N_GROUPS, SSM_STATE)
    return (y_p, y_s.reshape(n_dec, 1, D_MODEL),
            heads_k(k, (n_batch, seq)), heads_v(v, (n_batch, seq)),
            heads_k(ks, (n_dec, 1)), heads_v(vs, (n_dec, 1)),
            state(h_p[:, :N_STATE], n_batch), state(h_p[:, N_STATE:], n_batch),
            state(h_s[:, :N_STATE], n_dec), state(h_s[:, N_STATE:], n_dec),
            conv_p.reshape(1, n_batch, 2, D_FF), conv_s.reshape(1, n_dec, 2, D_FF))
```

```python
import functools
import math

import jax
import jax.numpy as jnp
from jax import lax
from jax.experimental import pallas as pl
from jax.experimental.pallas import tpu as pltpu

F32 = jnp.float32
BF16 = jnp.bfloat16

D_MODEL = 1024
N_HEADS = 4
HEAD_DIM = 64
V_DIM = 2 * HEAD_DIM
D_ATT = N_HEADS * V_DIM
D_QK = N_HEADS * 2 * HEAD_DIM
D_SSM = 512
SSM_GROUP = 16
N_GROUPS = D_SSM // SSM_GROUP
SSM_STATE = 64
N_STATE = N_GROUPS * SSM_STATE
D_IN = 2 * D_QK + D_ATT + D_SSM
D_FF = 2816
PAGE = 128
EPS = 1e-6
LAM_INIT = 0.8 - 0.6 * math.exp(-0.3 * 0)
QK_SCALE = HEAD_DIM ** -0.5

LANES = 128
SUBLANES = 8
MXU_COLS = 256
VMEM_LIMIT = 56 * 1024 * 1024

FF_CHUNK = MXU_COLS
N_FF_CHUNKS = D_FF // FF_CHUNK
SSM_CHUNK_GROUPS = MXU_COLS // SSM_GROUP
SSM_CHUNK_STATES = SSM_CHUNK_GROUPS * SSM_STATE
N_SSM_CHUNKS = D_SSM // MXU_COLS
SCAN_COLS = 512


def _params(n_axes, vmem=VMEM_LIMIT):
    return pltpu.CompilerParams(dimension_semantics=("arbitrary",) * n_axes, vmem_limit_bytes=vmem)


def _rms(x, g):
    return x * lax.rsqrt(jnp.mean(x * x, axis=-1, keepdims=True) + EPS) * g


def _sigmoid(x):
    return 1.0 / (1.0 + jnp.exp(-x))


def _diff_lambda(lamp_ref):
    lp = lamp_ref[...]
    a1 = jnp.sum(lp[0:1] * lp[1:2], axis=-1, keepdims=True)
    a2 = jnp.sum(lp[2:3] * lp[3:4], axis=-1, keepdims=True)
    return jnp.exp(a1) - jnp.exp(a2) + LAM_INIT


def _resident(shape):
    return pl.BlockSpec(shape, lambda *_: (0,) * len(shape), pipeline_mode=pl.Buffered(1))


def _disc_body(lr_ref, li_ref, ldt_ref, br_ref, bi_ref, ar_ref, ai_ref, bbr_ref, bbi_ref):
    lr, li = lr_ref[...], li_ref[...]
    dt = jnp.exp(ldt_ref[...])
    mag = jnp.exp(lr * dt)
    a_re, a_im = mag * jnp.cos(li * dt), mag * jnp.sin(li * dt)
    den = lr * lr + li * li
    nr, ni = a_re - 1.0, a_im
    coef_re = (nr * lr + ni * li) / den
    coef_im = (ni * lr - nr * li) / den
    br, bi = br_ref[...], bi_ref[...]
    bbr_ref[...] = coef_re * br - coef_im * bi
    bbi_ref[...] = coef_re * bi + coef_im * br
    ar_ref[...] = a_re
    ai_ref[...] = a_im


def _ssm_discretize(lam_re, lam_im, log_dt, b_re, b_im):
    row = jax.ShapeDtypeStruct((1, N_STATE), F32)
    mat = jax.ShapeDtypeStruct((SSM_GROUP, N_STATE), F32)
    return pl.pallas_call(_disc_body, out_shape=(row, row, mat, mat), name="ssm_discretize")(
        lam_re.reshape(1, N_STATE), lam_im.reshape(1, N_STATE),
        jnp.repeat(log_dt, SSM_STATE).reshape(1, N_STATE),
        b_re.transpose(2, 0, 1).reshape(SSM_GROUP, N_STATE),
        b_im.transpose(2, 0, 1).reshape(SSM_GROUP, N_STATE))


def _blockdiag_in(bbt):
    t = bbt.reshape(SSM_GROUP, N_SSM_CHUNKS, SSM_CHUNK_GROUPS, SSM_STATE)
    eye = jnp.eye(SSM_CHUNK_GROUPS, dtype=bbt.dtype)
    return jnp.einsum('cngp,gh->ngchp', t, eye).reshape(N_SSM_CHUNKS, MXU_COLS, SSM_CHUNK_STATES)


def _blockdiag_out(c):
    t = c.reshape(N_SSM_CHUNKS, SSM_CHUNK_GROUPS, SSM_GROUP, SSM_STATE)
    eye = jnp.eye(SSM_CHUNK_GROUPS, dtype=c.dtype)
    return jnp.einsum('ngkp,gh->ngphk', t, eye).reshape(N_SSM_CHUNKS, SSM_CHUNK_STATES, MXU_COLS)


def _inproj_body(x_ref, g_ref, w_ref, q_ref, k_ref, v_ref, u_ref, *maybe_bf16_kv):
    xn = _rms(x_ref[...], g_ref[...]).astype(BF16)

    def proj(lo):
        return jnp.dot(xn, w_ref[:, lo:lo + D_QK], preferred_element_type=F32)

    q_ref[...] = (proj(0) * QK_SCALE).astype(q_ref.dtype)
    k = proj(D_QK)
    v = proj(2 * D_QK)
    k_ref[...] = k
    for h in range(N_HEADS):
        v_ref[pl.ds(h, x_ref.shape[0], stride=N_HEADS), :] = v[:, h * V_DIM:(h + 1) * V_DIM]
    u_ref[...] = proj(2 * D_QK + D_ATT)
    if maybe_bf16_kv:
        kb_ref, vb_ref = maybe_bf16_kv
        kb_ref[...] = k.astype(BF16)
        vb_ref[...] = v.astype(BF16)


def _in_proj(x, g, w_bf, *, tm, q_dtype, bf16_kv):
    b, s, _ = x.shape
    tok = lambda dt: jax.ShapeDtypeStruct((b, s, D_QK), dt)
    tok_spec = pl.BlockSpec((None, tm, D_QK), lambda i, j: (i, j, 0))
    out_shape = [tok(q_dtype), tok(F32), jax.ShapeDtypeStruct((b, s * N_HEADS, V_DIM), F32),
                 jax.ShapeDtypeStruct((s, b * D_SSM), F32)]
    out_specs = [tok_spec, tok_spec, pl.BlockSpec((None, tm * N_HEADS, V_DIM), lambda i, j: (i, j, 0)),
                 pl.BlockSpec((tm, D_SSM), lambda i, j: (j, i))]
    if bf16_kv:
        out_shape += [tok(BF16), tok(BF16)]
        out_specs += [tok_spec, tok_spec]
    return pl.pallas_call(
        _inproj_body,
        grid=(b, s // tm),
        in_specs=[pl.BlockSpec((None, tm, D_MODEL), lambda i, j: (i, j, 0)),
                  _resident((1, D_MODEL)), _resident((D_MODEL, D_IN))],
        out_specs=out_specs, out_shape=out_shape,
        compiler_params=_params(2), name="in_proj")(x, g, w_bf)


POS_RADIX = 256


def _alibi_key_features(seq):
    j = lax.broadcasted_iota(jnp.int32, (seq, LANES), 0)
    lane = lax.broadcasted_iota(jnp.int32, (seq, LANES), 1)
    feat = jnp.where(lane == 0, j // POS_RADIX, jnp.where(lane == 1, j % POS_RADIX, 0))
    return feat.astype(BF16)


def _prompt_attn_body(lamp_ref, sg_ref, q_ref, k_ref, v_ref, pos_ref, o_ref, m_scr, acc_scr, qs_scr, s_scr,
                      *, tq, tk):
    qi = pl.program_id(1)
    lam = _diff_lambda(lamp_ref)
    lane = lax.broadcasted_iota(jnp.int32, (1, LANES), 1)
    n_full = (qi * tq) // tk
    diag_off = qi * tq - n_full * tk
    row_in_tile = lax.broadcasted_iota(jnp.int32, (2 * tq, tk), 0) % tq
    col = lax.broadcasted_iota(jnp.int32, (2 * tq, tk), 1)
    causal = row_in_tile + diag_off >= col
    ones_col = jnp.broadcast_to(jnp.where(lane == 0, 1.0, 0.0).astype(BF16), (tk, LANES))
    nt = (((1,), (1,)), ((), ()))

    heads = [slice(h * V_DIM, (h + 1) * V_DIM) for h in range(N_HEADS)]
    for h, hs in enumerate(heads):
        slope = 2.0 ** (-8.0 * (h + 1) / N_HEADS)
        q = q_ref[:, hs]
        zero = jnp.zeros_like(q)
        q_pos = jnp.broadcast_to(
            jnp.where(lane == 0, POS_RADIX * slope, jnp.where(lane == 1, slope, 0.0)).astype(BF16), (tq, LANES))
        qs_scr[h, :tq, :LANES] = jnp.where(lane < HEAD_DIM, q, zero)
        qs_scr[h, tq:, :LANES] = jnp.where(lane >= HEAD_DIM, q, zero)
        qs_scr[h, :tq, LANES:] = q_pos
        qs_scr[h, tq:, LANES:] = q_pos
    m_scr[...] = jnp.full(m_scr.shape, -jnp.inf, F32)
    acc_scr[...] = jnp.zeros(acc_scr.shape, F32)

    def kv_rows(j):
        return pl.ds(pl.multiple_of(j * tk, tk), tk)

    def scores(j):
        rows = kv_rows(j)
        pos = pos_ref[rows, :]
        for h, hs in enumerate(heads):
            kx = jnp.concatenate([k_ref[rows, hs], pos], axis=1)
            s_scr[j % 2, h] = lax.dot_general(qs_scr[h], kx, nt, preferred_element_type=F32)

    def absorb(j, masked):
        rows = kv_rows(j)
        for h, hs in enumerate(heads):
            vx = jnp.concatenate([v_ref[rows, hs], ones_col], axis=1)
            s = s_scr[j % 2, h]
            if masked:
                s = jnp.where(causal, s, -jnp.inf)
            m_old = m_scr[h]
            m_new = jnp.maximum(m_old, jnp.max(s, axis=-1, keepdims=True))
            p = jnp.exp(s - m_new).astype(BF16)
            acc_scr[h] = (jnp.exp(m_old - m_new) * acc_scr[h]
                          + jnp.dot(p, vx, preferred_element_type=F32))
            m_scr[h] = m_new

    def pipelined(j, carry):
        absorb(j, False)
        scores(j + 1)
        return carry

    scores(0)
    lax.fori_loop(0, n_full, pipelined, 0)
    absorb(n_full, True)
    for h, hs in enumerate(heads):
        acc = acc_scr[h]
        a1, a2 = acc[:tq], acc[tq:]
        o = (a1[:, :V_DIM] / a1[:, V_DIM:V_DIM + 1]
             - lam * (a2[:, :V_DIM] / a2[:, V_DIM:V_DIM + 1]))
        o_ref[:, hs] = (_rms(o, sg_ref[...]) * (1.0 - LAM_INIT)).astype(o_ref.dtype)


def _prompt_attention(lamp, subln_g, q, k, v, *, tq, tk):
    b, s, _ = q.shape
    assert s % tk == 0 and tk % tq == 0
    seq_spec = pl.BlockSpec((None, s, D_QK), lambda i, j: (i, 0, 0))
    return pl.pallas_call(
        functools.partial(_prompt_attn_body, tq=tq, tk=tk),
        grid=(b, s // tq),
        in_specs=[_resident((4, HEAD_DIM)), _resident((1, V_DIM)),
                  pl.BlockSpec((None, tq, D_QK), lambda i, j: (i, j, 0)), seq_spec, seq_spec,
                  _resident((s, LANES))],
        out_specs=pl.BlockSpec((None, tq, D_ATT), lambda i, j: (i, j, 0)),
        out_shape=jax.ShapeDtypeStruct((b, s, D_ATT), BF16),
        scratch_shapes=[pltpu.VMEM((N_HEADS, 2 * tq, 1), F32), pltpu.VMEM((N_HEADS, 2 * tq, 2 * LANES), F32),
                        pltpu.VMEM((N_HEADS, 2 * tq, 2 * LANES), BF16),
                        pltpu.VMEM((2, N_HEADS, 2 * tq, tk), F32)],
        compiler_params=_params(2), name="prompt_attn")(lamp, subln_g, q, k, v, _alibi_key_features(s))


N_MAPS = 2 * N_HEADS


def _decode_scratch(pages_per_chunk):
    chunk_tokens = pages_per_chunk * PAGE
    return [pltpu.VMEM((2, D_QK, chunk_tokens), F32),
            pltpu.VMEM((2, chunk_tokens * N_HEADS, V_DIM), F32),
            pltpu.SemaphoreType.DMA((2, 2)),
            pltpu.VMEM((N_MAPS, 1), F32), pltpu.VMEM((N_MAPS, 1), F32), pltpu.VMEM((N_MAPS, D_ATT), F32)]


def _make_decode_stream(pt_ref, lamp_ref, sg_ref, q_ref, kn_ref, vn_ref, ck_hbm, cv_hbm, o_ref,
                        kbuf, vbuf, sem, m_ref, l_ref, acc_ref, *, tile, n_tiles, per_tile, n_chunks,
                        pages_per_chunk, n_pages):
    assert per_tile % n_chunks == 0 and per_tile % 2 == 0
    seqs_per_tile = per_tile // n_chunks
    chunk_tokens = pages_per_chunk * PAGE
    past_len = n_pages * PAGE
    row = lax.broadcasted_iota(jnp.int32, (N_MAPS, D_QK), 0)
    lane = lax.broadcasted_iota(jnp.int32, (N_MAPS, D_QK), 1)
    row1 = lax.broadcasted_iota(jnp.int32, (N_MAPS, 1), 0)
    slope = jnp.exp2(-8.0 * ((row1 // 2) + 1).astype(F32) / N_HEADS)
    heads = [slice(h * V_DIM, (h + 1) * V_DIM) for h in range(N_HEADS)]

    def coords(c):
        return tile * seqs_per_tile + c // n_chunks, c % n_chunks, c % 2

    def page_copies(seq, chunk, slot):
        copies = []
        for p in range(pages_per_chunk):
            page = pt_ref[seq * n_pages + chunk * pages_per_chunk + p]
            copies.append(pltpu.make_async_copy(
                ck_hbm.at[page], kbuf.at[slot, :, pl.ds(p * PAGE, PAGE)], sem.at[0, slot]))
            copies.append(pltpu.make_async_copy(
                cv_hbm.at[page], vbuf.at[slot, pl.ds(p * PAGE * N_HEADS, PAGE * N_HEADS), :], sem.at[1, slot]))
        return copies

    def start(copies):
        for cp in copies:
            cp.start()

    def decode_chunk(c):
        seq, chunk, slot = coords(c)
        if c == 0:
            @pl.when(tile == 0)
            def _():
                start(page_copies(*coords(0)))
        if c + 1 < per_tile:
            start(page_copies(*coords(c + 1)))
        else:
            @pl.when(tile + 1 < n_tiles)
            def _():
                start(page_copies((tile + 1) * seqs_per_tile, 0, 0))
        for cp in page_copies(seq, chunk, slot):
            cp.wait()

        q = q_ref[pl.ds(seq, 1), :]
        qblk = jnp.where(lane // HEAD_DIM == row, jnp.broadcast_to(q, (N_MAPS, D_QK)), 0.0)
        kt = kbuf[slot].astype(BF16)
        s = jnp.dot(qblk.astype(BF16), kt, preferred_element_type=F32)
        pos = chunk * chunk_tokens + lax.broadcasted_iota(jnp.int32, (1, chunk_tokens), 1)
        s = s - slope * (past_len - pos).astype(F32)
        s_max = jnp.max(s, axis=-1, keepdims=True)
        first, last = chunk == 0, chunk == n_chunks - 1
        m_new = s_max if first else jnp.maximum(m_ref[...], s_max)
        p = jnp.exp(s - m_new)
        l_new = jnp.sum(p, axis=-1, keepdims=True)
        if not first:
            alpha = jnp.exp(m_ref[...] - m_new)
            l_new = alpha * l_ref[...] + l_new
        pb = p.astype(BF16)
        acc = []
        for h, hs in enumerate(heads):
            vh = vbuf[slot, pl.ds(h, chunk_tokens, stride=N_HEADS), :].astype(BF16)
            pv = jnp.dot(pb, vh, preferred_element_type=F32)
            acc.append(pv if first else alpha * acc_ref[:, hs] + pv)
        if not last:
            m_ref[...] = m_new
            l_ref[...] = l_new
            for hs, a in zip(heads, acc):
                acc_ref[:, hs] = a
            return
        s_new = jnp.sum(qblk * kn_ref[pl.ds(seq, 1), :], axis=-1, keepdims=True)
        m_fin = jnp.maximum(m_new, s_new)
        alpha = jnp.exp(m_new - m_fin)
        p_new = jnp.exp(s_new - m_fin)
        l_fin = alpha * l_new + p_new
        acc = alpha * jnp.concatenate(acc, axis=-1) + p_new * vn_ref[pl.ds(seq, 1), :]
        lam = _diff_lambda(lamp_ref)
        coef = jnp.where(row1 % 2 == 0, 1.0, -lam) / l_fin
        own = lane // V_DIM == row // 2
        o = jnp.sum(jnp.where(own, coef * acc, 0.0), axis=0, keepdims=True)
        o_ref[pl.ds(seq, 1), :] = jnp.concatenate(
            [_rms(o[:, hs], sg_ref[...]) * (1.0 - LAM_INIT) for hs in heads], axis=-1)

    return decode_chunk


def _ssm_body(u_ref, h0_ref, ar_ref, ai_ref, bre_ref, bim_ref, cre_ref, cim_ref, d_ref, wg_ref, bg_ref,
              s_ref, h_ref, x_scr, *, rows_per_step, n_steps):
    r = rows_per_step

    @pl.when(pl.program_id(0) == 0)
    def _():
        h_ref[...] = h0_ref[...]

    u = u_ref[...]
    ub = u.astype(BF16)
    for n in range(N_SSM_CHUNKS):
        un = ub[:, n * MXU_COLS:(n + 1) * MXU_COLS]
        cols = slice(n * SSM_CHUNK_STATES, (n + 1) * SSM_CHUNK_STATES)
        x_scr[:, cols] = jnp.dot(un, bre_ref[n], preferred_element_type=F32)
        x_scr[:, N_STATE + n * SSM_CHUNK_STATES:N_STATE + (n + 1) * SSM_CHUNK_STATES] = jnp.dot(
            un, bim_ref[n], preferred_element_type=F32)

    for cb in range(N_STATE // SCAN_COLS):
        re = slice(cb * SCAN_COLS, (cb + 1) * SCAN_COLS)
        im = slice(N_STATE + cb * SCAN_COLS, N_STATE + (cb + 1) * SCAN_COLS)
        a_r = jnp.broadcast_to(ar_ref[:, re], (r, SCAN_COLS))
        a_i = jnp.broadcast_to(ai_ref[:, re], (r, SCAN_COLS))

        def step(t, carry, re=re, im=im, a_r=a_r, a_i=a_i):
            hr, hi = carry
            rows = pl.ds(pl.multiple_of(t * r, r), r)
            nhr = a_r * hr - a_i * hi + x_scr[rows, re]
            nhi = a_r * hi + a_i * hr + x_scr[rows, im]
            x_scr[rows, re] = nhr
            x_scr[rows, im] = nhi
            return nhr, nhi

        carry = (h_ref[:, re], h_ref[:, im])
        if n_steps == 1:
            hr, hi = step(0, carry)
        else:
            hr, hi = lax.fori_loop(0, n_steps, step, carry, unroll=8)
        h_ref[:, re] = hr
        h_ref[:, im] = hi

    ys = []
    for n in range(N_SSM_CHUNKS):
        cols = slice(n * SSM_CHUNK_STATES, (n + 1) * SSM_CHUNK_STATES)
        cols_im = slice(N_STATE + n * SSM_CHUNK_STATES, N_STATE + (n + 1) * SSM_CHUNK_STATES)
        ys.append(jnp.dot(x_scr[:, cols].astype(BF16), cre_ref[n], preferred_element_type=F32)
                  - jnp.dot(x_scr[:, cols_im].astype(BF16), cim_ref[n], preferred_element_type=F32))
    y = jnp.concatenate(ys, axis=-1) + d_ref[...] * u
    g = 0.5 * y * (1.0 + jnp.tanh(math.sqrt(2.0 / math.pi) * (y + 0.044715 * (y * y * y))))
    z = jnp.dot(g.astype(BF16), wg_ref[...], preferred_element_type=F32) + bg_ref[...]
    s_ref[...] = (g * _sigmoid(z)).astype(s_ref.dtype)


def _ssm_glu(u, h0, a_re, a_im, b_re_bd, b_im_bd, c_re_bd, c_im_bd, d_skip, w_glu_bf, b_glu,
             *, rows_per_step, steps_per_block):
    n_rows = u.shape[0]
    blk = rows_per_step * steps_per_block
    body = functools.partial(_ssm_body, rows_per_step=rows_per_step, n_steps=steps_per_block)
    return pl.pallas_call(
        body,
        grid=(n_rows // blk,),
        in_specs=[pl.BlockSpec((blk, D_SSM), lambda i: (i, 0)),
                  _resident((rows_per_step, 2 * N_STATE)),
                  _resident((1, N_STATE)), _resident((1, N_STATE)),
                  _resident((N_SSM_CHUNKS, MXU_COLS, SSM_CHUNK_STATES)),
                  _resident((N_SSM_CHUNKS, MXU_COLS, SSM_CHUNK_STATES)),
                  _resident((N_SSM_CHUNKS, SSM_CHUNK_STATES, MXU_COLS)),
                  _resident((N_SSM_CHUNKS, SSM_CHUNK_STATES, MXU_COLS)),
                  _resident((1, D_SSM)), _resident((D_SSM, D_SSM)), _resident((1, D_SSM))],
        out_specs=[pl.BlockSpec((blk, D_SSM), lambda i: (i, 0)),
                   pl.BlockSpec((rows_per_step, 2 * N_STATE), lambda i: (0, 0))],
        out_shape=[jax.ShapeDtypeStruct((n_rows, D_SSM), BF16),
                   jax.ShapeDtypeStruct((rows_per_step, 2 * N_STATE), F32)],
        scratch_shapes=[pltpu.VMEM((blk, 2 * N_STATE), F32)],
        compiler_params=_params(1), name="ssm_glu")(
            u, h0, a_re, a_im, b_re_bd, b_im_bd, c_re_bd, c_im_bd, d_skip, w_glu_bf, b_glu)


def _out_ffn_body(x_ref, o_ref, s_ref, cprev_ref, wo_ref, g2_ref, wup_ref, cw_ref, cb_ref, wdn_ref, gf_ref,
                  y_ref, cout_ref, acc_ref, *, rows_are_positions, before_chunk=None):
    tm = x_ref.shape[0]
    x1 = (x_ref[...]
          + jnp.dot(o_ref[...], wo_ref[:D_ATT, :], preferred_element_type=F32)
          + jnp.dot(s_ref[...], wo_ref[D_ATT:, :], preferred_element_type=F32))
    xn = _rms(x1, g2_ref[...]).astype(BF16)

    if rows_are_positions:
        @pl.when(pl.program_id(1) == 0)
        def _():
            cout_ref[...] = cprev_ref[...]
        hist = cout_ref[...]
        rid = lax.broadcasted_iota(jnp.int32, (tm, FF_CHUNK), 0)
    else:
        hist = cprev_ref[...]

    acc_ref[...] = jnp.zeros(acc_ref.shape, F32)
    for c in range(N_FF_CHUNKS):
        if before_chunk is not None:
            before_chunk(c)
        cs = slice(c * FF_CHUNK, (c + 1) * FF_CHUNK)
        cs_up = slice(D_FF + c * FF_CHUNK, D_FF + (c + 1) * FF_CHUNK)
        gate = jnp.dot(xn, wup_ref[:, cs], preferred_element_type=F32)
        up = jnp.dot(xn, wup_ref[:, cs_up], preferred_element_type=F32)
        if rows_are_positions:
            p0, p1 = hist[0:1, cs], hist[1:2, cs]
            g1 = jnp.where(rid == 0, p1, pltpu.roll(gate, 1, 0))
            g2 = jnp.where(rid == 0, p0, jnp.where(rid == 1, p1, pltpu.roll(gate, 2, 0)))
            cout_ref[:, cs] = gate[tm - 2:, :]
        else:
            g2, g1 = hist[:, cs], hist[:, cs_up]
            cout_ref[:, cs] = g1
            cout_ref[:, cs_up] = gate
        gc = cb_ref[:, cs] + cw_ref[0:1, cs] * g2 + cw_ref[1:2, cs] * g1 + cw_ref[2:3, cs] * gate
        act = (gc * _sigmoid(gc) * up).astype(BF16)
        acc_ref[...] += jnp.dot(act, wdn_ref[cs, :], preferred_element_type=F32)
    y_ref[...] = _rms(x1 + acc_ref[...], gf_ref[...])


def _out_ffn_specs(x, conv_prev, tm):
    conv_spec = pl.BlockSpec((None,) + conv_prev.shape[1:], lambda i, j, *_: (i, 0, 0))
    tok_spec = lambda width: pl.BlockSpec((None, tm, width), lambda i, j, *_: (i, j, 0))
    in_specs = [tok_spec(D_MODEL), tok_spec(D_ATT), pl.BlockSpec((tm, D_SSM), lambda i, j, *_: (j, i)), conv_spec,
                _resident((D_MODEL, D_MODEL)), _resident((1, D_MODEL)),
                _resident((D_MODEL, 2 * D_FF)), _resident((3, D_FF)), _resident((1, D_FF)),
                _resident((D_FF, D_MODEL)), _resident((1, D_MODEL))]
    out_specs = [tok_spec(D_MODEL), conv_spec]
    out_shape = [jax.ShapeDtypeStruct(x.shape, F32), jax.ShapeDtypeStruct(conv_prev.shape, F32)]
    return in_specs, out_specs, out_shape, [pltpu.VMEM((tm, D_MODEL), F32)]


def _out_ffn(x, o, s, conv_prev, *ffn_w, tm, rows_are_positions):
    in_specs, out_specs, out_shape, scratch = _out_ffn_specs(x, conv_prev, tm)
    return pl.pallas_call(
        functools.partial(_out_ffn_body, rows_are_positions=rows_are_positions),
        grid=(x.shape[0], x.shape[1] // tm),
        in_specs=in_specs, out_specs=out_specs, out_shape=out_shape, scratch_shapes=scratch,
        compiler_params=_params(2), name="out_ffn")(x, o, s, conv_prev, *ffn_w)


N_FFN_REFS = 11


def _out_ffn_decode_body(pt_ref, *refs, per_tile, n_chunks, pages_per_chunk, n_pages):
    ffn_in, refs = refs[:N_FFN_REFS], refs[N_FFN_REFS:]
    (lamp_ref, sg_ref, q_ref, kn_ref, vn_ref, ck_hbm, cv_hbm), refs = refs[:7], refs[7:]
    (y_ref, cout_ref, od_ref, acc_ref), dec_scratch = refs[:4], refs[4:]
    tile = pl.program_id(0) * pl.num_programs(1) + pl.program_id(1)
    decode_chunk = _make_decode_stream(
        pt_ref, lamp_ref, sg_ref, q_ref, kn_ref, vn_ref, ck_hbm, cv_hbm, od_ref, *dec_scratch,
        tile=tile, n_tiles=pl.num_programs(0) * pl.num_programs(1), per_tile=per_tile, n_chunks=n_chunks,
        pages_per_chunk=pages_per_chunk, n_pages=n_pages)

    decode_before = {(d * N_FF_CHUNKS) // per_tile: d for d in range(per_tile)}

    def before_chunk(c):
        if c in decode_before:
            decode_chunk(decode_before[c])

    _out_ffn_body(*ffn_in, y_ref, cout_ref, acc_ref, rows_are_positions=True, before_chunk=before_chunk)


def _out_ffn_with_decode(x, o, s, conv_prev, ffn_w, page_table, lamp, subln_g, q, k_new, v_new,
                         cache_k, cache_v, *, tm, pages_per_chunk):
    n_tiles = x.shape[0] * (x.shape[1] // tm)
    n_dec, n_pages = page_table.shape
    n_chunks = n_pages // pages_per_chunk
    per_tile, rem = divmod(n_dec * n_chunks, n_tiles)
    assert rem == 0 and per_tile <= N_FF_CHUNKS
    in_specs, out_specs, out_shape, scratch = _out_ffn_specs(x, conv_prev, tm)
    seq_rows = _resident((n_dec, D_QK))
    hbm = pl.BlockSpec(memory_space=pl.ANY)
    grid_spec = pltpu.PrefetchScalarGridSpec(
        num_scalar_prefetch=1,
        grid=(x.shape[0], x.shape[1] // tm),
        in_specs=in_specs + [_resident((4, HEAD_DIM)), _resident((1, V_DIM)), seq_rows, seq_rows, seq_rows, hbm, hbm],
        out_specs=out_specs + [pl.BlockSpec((n_dec, D_ATT), lambda *_: (0, 0))],
        scratch_shapes=scratch + _decode_scratch(pages_per_chunk))
    body = functools.partial(_out_ffn_decode_body, per_tile=per_tile, n_chunks=n_chunks,
                             pages_per_chunk=pages_per_chunk, n_pages=n_pages)
    return pl.pallas_call(
        body, grid_spec=grid_spec, out_shape=out_shape + [jax.ShapeDtypeStruct((n_dec, D_ATT), F32)],
        compiler_params=_params(2), name="out_ffn_decode")(
            page_table.reshape(-1), x, o, s, conv_prev, *ffn_w, lamp, subln_g, q, k_new, v_new, cache_k, cache_v)


def kernel(x_prompt, x_sample, cache_k, cache_v, state_ssm_re, state_ssm_im, cache_ffn_conv, page_table,
           norm1_g, w_in, lam_q1, lam_k1, lam_q2, lam_k2, subln_g, ssm_lam_re, ssm_lam_im, ssm_log_dt,
           ssm_b_re, ssm_b_im, ssm_c_re, ssm_c_im, ssm_d, w_glu, b_glu, w_out, norm2_g, w_up, conv_w,
           conv_b, w_down, final_g):
    assert w_in.shape[0] == 1, "single-layer step"
    n_batch, seq, _ = x_prompt.shape
    n_dec = x_sample.shape[0]
    n_pool = cache_k.shape[1]

    w_in_bf, w_out_bf = w_in[0].astype(BF16), w_out[0].astype(BF16)
    w_up_bf, w_down_bf, w_glu_bf = w_up[0].astype(BF16), w_down[0].astype(BF16), w_glu[0].astype(BF16)
    g1, g2, gf = norm1_g[0].reshape(1, -1), norm2_g[0].reshape(1, -1), final_g.reshape(1, -1)
    lamp = jnp.stack([lam_q1[0], lam_k1[0], lam_q2[0], lam_k2[0]])
    sg = subln_g[0].reshape(1, V_DIM)
    a_re, a_im, bb_re, bb_im = _ssm_discretize(ssm_lam_re[0], ssm_lam_im[0], ssm_log_dt[0],
                                               ssm_b_re[0], ssm_b_im[0])
    ssm_w = (a_re, a_im, _blockdiag_in(bb_re).astype(BF16), _blockdiag_in(bb_im).astype(BF16),
             _blockdiag_out(ssm_c_re[0]).astype(BF16), _blockdiag_out(ssm_c_im[0]).astype(BF16),
             ssm_d[0].reshape(1, -1), w_glu_bf, b_glu[0].reshape(1, -1))
    ffn_w = (w_out_bf, g2, w_up_bf, conv_w[0], conv_b[0].reshape(1, -1), w_down_bf, gf)

    q, k, v, u, k_bf, v_bf = _in_proj(x_prompt, g1, w_in_bf, tm=512, q_dtype=BF16, bf16_kv=True)
    o = _prompt_attention(lamp, sg, q, k_bf, v_bf, tq=256, tk=512)
    s, h_p = _ssm_glu(u.reshape(seq * n_batch, D_SSM), jnp.zeros((n_batch, 2 * N_STATE), F32), *ssm_w,
                      rows_per_step=n_batch, steps_per_block=64)

    xs = x_sample.reshape(1, n_dec, D_MODEL)
    qs, ks, vs, us = _in_proj(xs, g1, w_in_bf, tm=n_dec, q_dtype=F32, bf16_kv=False)
    seq_rows = lambda t: t.reshape(n_dec, D_QK)
    y_p, conv_p, o_s = _out_ffn_with_decode(
        x_prompt, o, s.reshape(seq, n_batch * D_SSM), jnp.zeros((n_batch, 2, D_FF), F32), ffn_w,
        page_table, lamp, sg, seq_rows(qs), seq_rows(ks), seq_rows(vs),
        cache_k.transpose(0, 1, 3, 4, 5, 2).reshape(n_pool, D_QK, PAGE),
        cache_v.reshape(n_pool, PAGE * N_HEADS, V_DIM),
        tm=256, pages_per_chunk=16)
    h0_s = jnp.concatenate([state_ssm_re[0].reshape(n_dec, N_STATE),
                            state_ssm_im[0].reshape(n_dec, N_STATE)], axis=-1)
    s_s, h_s = _ssm_glu(us, h0_s, *ssm_w, rows_per_step=n_dec, steps_per_block=1)
    y_s, conv_s = _out_ffn(xs, o_s.reshape(1, n_dec, D_ATT).astype(BF16), s_s,
                           cache_ffn_conv[0].reshape(1, n_dec, 2 * D_FF), *ffn_w,
                           tm=n_dec, rows_are_positions=False)

    heads_k = lambda t, lead: t.reshape(1, *lead, N_HEADS, 2, HEAD_DIM)
    heads_v = lambda t, lead: t.reshape(1, *lead, N_HEADS, V_DIM)
    state = lambda h, n: h.reshape(1, n, N_GROUPS, SSM_STATE)
    return (y_p, y_s.reshape(n_dec, 1, D_MODEL),
            heads_k(k, (n_batch, seq)), heads_v(v, (n_batch, seq)),
            heads_k(ks, (n_dec, 1)), heads_v(vs, (n_dec, 1)),
            state(h_p[:, :N_STATE], n_batch), state(h_p[:, N_STATE:], n_batch),
            state(h_s[:, :N_STATE], n_dec), state(h_s[:, N_STATE:], n_dec),
            conv_p.reshape(1, n_batch, 2, D_FF), conv_s.reshape(1, n_dec, 2, D_FF))
```

```python
import functools
import math

import jax
import jax.numpy as jnp
from jax import lax
from jax.experimental import pallas as pl
from jax.experimental.pallas import tpu as pltpu

F32 = jnp.float32
BF16 = jnp.bfloat16

D_MODEL = 1024
N_HEADS = 4
HEAD_DIM = 64
V_DIM = 2 * HEAD_DIM
D_ATT = N_HEADS * V_DIM
D_QK = N_HEADS * 2 * HEAD_DIM
D_SSM = 512
SSM_GROUP = 16
N_GROUPS = D_SSM // SSM_GROUP
SSM_STATE = 64
N_STATE = N_GROUPS * SSM_STATE
D_IN = 2 * D_QK + D_ATT + D_SSM
D_FF = 2816
PAGE = 128
EPS = 1e-6
LAM_INIT = 0.8 - 0.6 * math.exp(-0.3 * 0)
QK_SCALE = HEAD_DIM ** -0.5

LANES = 128
SUBLANES = 8
MXU_COLS = 256
VMEM_LIMIT = 56 * 1024 * 1024

FF_CHUNK = MXU_COLS
N_FF_CHUNKS = D_FF // FF_CHUNK
SSM_CHUNK_GROUPS = MXU_COLS // SSM_GROUP
SSM_CHUNK_STATES = SSM_CHUNK_GROUPS * SSM_STATE
N_SSM_CHUNKS = D_SSM // MXU_COLS
SCAN_COLS = 512


def _params(n_axes, vmem=VMEM_LIMIT):
    return pltpu.CompilerParams(dimension_semantics=("arbitrary",) * n_axes, vmem_limit_bytes=vmem)


def _rms(x, g):
    return x * lax.rsqrt(jnp.mean(x * x, axis=-1, keepdims=True) + EPS) * g


def _sigmoid(x):
    return 1.0 / (1.0 + jnp.exp(-x))


def _diff_lambda(lamp_ref):
    lp = lamp_ref[...]
    a1 = jnp.sum(lp[0:1] * lp[1:2], axis=-1, keepdims=True)
    a2 = jnp.sum(lp[2:3] * lp[3:4], axis=-1, keepdims=True)
    return jnp.exp(a1) - jnp.exp(a2) + LAM_INIT


def _resident(shape):
    return pl.BlockSpec(shape, lambda *_: (0,) * len(shape), pipeline_mode=pl.Buffered(1))


def _disc_body(lr_ref, li_ref, ldt_ref, br_ref, bi_ref, ar_ref, ai_ref, bbr_ref, bbi_ref):
    lr, li = lr_ref[...], li_ref[...]
    dt = jnp.exp(ldt_ref[...])
    mag = jnp.exp(lr * dt)
    a_re, a_im = mag * jnp.cos(li * dt), mag * jnp.sin(li * dt)
    den = lr * lr + li * li
    nr, ni = a_re - 1.0, a_im
    coef_re = (nr * lr + ni * li) / den
    coef_im = (ni * lr - nr * li) / den
    br, bi = br_ref[...], bi_ref[...]
    bbr_ref[...] = coef_re * br - coef_im * bi
    bbi_ref[...] = coef_re * bi + coef_im * br
    ar_ref[...] = a_re
    ai_ref[...] = a_im


def _ssm_discretize(lam_re, lam_im, log_dt, b_re, b_im):
    row = jax.ShapeDtypeStruct((1, N_STATE), F32)
    mat = jax.ShapeDtypeStruct((SSM_GROUP, N_STATE), F32)
    return pl.pallas_call(_disc_body, out_shape=(row, row, mat, mat), name="ssm_discretize")(
        lam_re.reshape(1, N_STATE), lam_im.reshape(1, N_STATE),
        jnp.repeat(log_dt, SSM_STATE).reshape(1, N_STATE),
        b_re.transpose(2, 0, 1).reshape(SSM_GROUP, N_STATE),
        b_im.transpose(2, 0, 1).reshape(SSM_GROUP, N_STATE))


def _blockdiag_in(bbt):
    t = bbt.reshape(SSM_GROUP, N_SSM_CHUNKS, SSM_CHUNK_GROUPS, SSM_STATE)
    eye = jnp.eye(SSM_CHUNK_GROUPS, dtype=bbt.dtype)
    return jnp.einsum('cngp,gh->ngchp', t, eye).reshape(N_SSM_CHUNKS, MXU_COLS, SSM_CHUNK_STATES)


def _blockdiag_out(c):
    t = c.reshape(N_SSM_CHUNKS, SSM_CHUNK_GROUPS, SSM_GROUP, SSM_STATE)
    eye = jnp.eye(SSM_CHUNK_GROUPS, dtype=c.dtype)
    return jnp.einsum('ngkp,gh->ngphk', t, eye).reshape(N_SSM_CHUNKS, SSM_CHUNK_STATES, MXU_COLS)


def _inproj_body(x_ref, g_ref, w_ref, q_ref, k_ref, v_ref, u_ref, *maybe_bf16_kv):
    xn = _rms(x_ref[...], g_ref[...]).astype(BF16)

    def proj(lo):
        return jnp.dot(xn, w_ref[:, lo:lo + D_QK], preferred_element_type=F32)

    q_ref[...] = (proj(0) * QK_SCALE).astype(q_ref.dtype)
    k = proj(D_QK)
    v = proj(2 * D_QK)
    k_ref[...] = k
    for h in range(N_HEADS):
        v_ref[pl.ds(h, x_ref.shape[0], stride=N_HEADS), :] = v[:, h * V_DIM:(h + 1) * V_DIM]
    u_ref[...] = proj(2 * D_QK + D_ATT)
    if maybe_bf16_kv:
        kb_ref, vb_ref = maybe_bf16_kv
        kb_ref[...] = k.astype(BF16)
        vb_ref[...] = v.astype(BF16)


def _in_proj(x, g, w_bf, *, tm, q_dtype, bf16_kv):
    b, s, _ = x.shape
    tok = lambda dt: jax.ShapeDtypeStruct((b, s, D_QK), dt)
    tok_spec = pl.BlockSpec((None, tm, D_QK), lambda i, j: (i, j, 0))
    out_shape = [tok(q_dtype), tok(F32), jax.ShapeDtypeStruct((b, s * N_HEADS, V_DIM), F32), tok(F32)]
    out_specs = [tok_spec, tok_spec, pl.BlockSpec((None, tm * N_HEADS, V_DIM), lambda i, j: (i, j, 0)), tok_spec]
    if bf16_kv:
        out_shape += [tok(BF16), tok(BF16)]
        out_specs += [tok_spec, tok_spec]
    return pl.pallas_call(
        _inproj_body,
        grid=(b, s // tm),
        in_specs=[pl.BlockSpec((None, tm, D_MODEL), lambda i, j: (i, j, 0)),
                  _resident((1, D_MODEL)), _resident((D_MODEL, D_IN))],
        out_specs=out_specs, out_shape=out_shape,
        compiler_params=_params(2), name="in_proj")(x, g, w_bf)


POS_RADIX = 256


def _alibi_key_features(seq):
    j = lax.broadcasted_iota(jnp.int32, (seq, LANES), 0)
    lane = lax.broadcasted_iota(jnp.int32, (seq, LANES), 1)
    feat = jnp.where(lane == 0, j // POS_RADIX, jnp.where(lane == 1, j % POS_RADIX, 0))
    return feat.astype(BF16)


def _prompt_attn_body(lamp_ref, sg_ref, q_ref, k_ref, v_ref, pos_ref, o_ref, m_scr, acc_scr, qs_scr, s_scr,
                      *, tq, tk):
    qi = pl.program_id(1)
    lam = _diff_lambda(lamp_ref)
    lane = lax.broadcasted_iota(jnp.int32, (1, LANES), 1)
    n_full = (qi * tq) // tk
    diag_off = qi * tq - n_full * tk
    row_in_tile = lax.broadcasted_iota(jnp.int32, (2 * tq, tk), 0) % tq
    col = lax.broadcasted_iota(jnp.int32, (2 * tq, tk), 1)
    causal = row_in_tile + diag_off >= col
    ones_col = jnp.broadcast_to(jnp.where(lane == 0, 1.0, 0.0).astype(BF16), (tk, LANES))
    nt = (((1,), (1,)), ((), ()))

    heads = [slice(h * V_DIM, (h + 1) * V_DIM) for h in range(N_HEADS)]
    for h, hs in enumerate(heads):
        slope = 2.0 ** (-8.0 * (h + 1) / N_HEADS)
        q = q_ref[:, hs]
        zero = jnp.zeros_like(q)
        q_pos = jnp.broadcast_to(
            jnp.where(lane == 0, POS_RADIX * slope, jnp.where(lane == 1, slope, 0.0)).astype(BF16), (tq, LANES))
        qs_scr[h, :tq, :LANES] = jnp.where(lane < HEAD_DIM, q, zero)
        qs_scr[h, tq:, :LANES] = jnp.where(lane >= HEAD_DIM, q, zero)
        qs_scr[h, :tq, LANES:] = q_pos
        qs_scr[h, tq:, LANES:] = q_pos
    m_scr[...] = jnp.full(m_scr.shape, -jnp.inf, F32)
    acc_scr[...] = jnp.zeros(acc_scr.shape, F32)

    def kv_rows(j):
        return pl.ds(pl.multiple_of(j * tk, tk), tk)

    def scores(j):
        rows = kv_rows(j)
        pos = pos_ref[rows, :]
        for h, hs in enumerate(heads):
            kx = jnp.concatenate([k_ref[rows, hs], pos], axis=1)
            s_scr[j % 2, h] = lax.dot_general(qs_scr[h], kx, nt, preferred_element_type=F32)

    def absorb(j, masked):
        rows = kv_rows(j)
        for h, hs in enumerate(heads):
            vx = jnp.concatenate([v_ref[rows, hs], ones_col], axis=1)
            s = s_scr[j % 2, h]
            if masked:
                s = jnp.where(causal, s, -jnp.inf)
            m_old = m_scr[h]
            m_new = jnp.maximum(m_old, jnp.max(s, axis=-1, keepdims=True))
            p = jnp.exp(s - m_new).astype(BF16)
            acc_scr[h] = (jnp.exp(m_old - m_new) * acc_scr[h]
                          + jnp.dot(p, vx, preferred_element_type=F32))
            m_scr[h] = m_new

    def pipelined(j, carry):
        absorb(j, False)
        scores(j + 1)
        return carry

    scores(0)
    lax.fori_loop(0, n_full, pipelined, 0)
    absorb(n_full, True)
    for h, hs in enumerate(heads):
        acc = acc_scr[h]
        a1, a2 = acc[:tq], acc[tq:]
        o = (a1[:, :V_DIM] / a1[:, V_DIM:V_DIM + 1]
             - lam * (a2[:, :V_DIM] / a2[:, V_DIM:V_DIM + 1]))
        o_ref[:, hs] = (_rms(o, sg_ref[...]) * (1.0 - LAM_INIT)).astype(o_ref.dtype)


def _prompt_attention(lamp, subln_g, q, k, v, *, tq, tk):
    b, s, _ = q.shape
    assert s % tk == 0 and tk % tq == 0
    seq_spec = pl.BlockSpec((None, s, D_QK), lambda i, j: (i, 0, 0))
    return pl.pallas_call(
        functools.partial(_prompt_attn_body, tq=tq, tk=tk),
        grid=(b, s // tq),
        in_specs=[_resident((4, HEAD_DIM)), _resident((1, V_DIM)),
                  pl.BlockSpec((None, tq, D_QK), lambda i, j: (i, j, 0)), seq_spec, seq_spec,
                  _resident((s, LANES))],
        out_specs=pl.BlockSpec((None, tq, D_ATT), lambda i, j: (i, j, 0)),
        out_shape=jax.ShapeDtypeStruct((b, s, D_ATT), BF16),
        scratch_shapes=[pltpu.VMEM((N_HEADS, 2 * tq, 1), F32), pltpu.VMEM((N_HEADS, 2 * tq, 2 * LANES), F32),
                        pltpu.VMEM((N_HEADS, 2 * tq, 2 * LANES), BF16),
                        pltpu.VMEM((2, N_HEADS, 2 * tq, tk), F32)],
        compiler_params=_params(2), name="prompt_attn")(lamp, subln_g, q, k, v, _alibi_key_features(s))


N_MAPS = 2 * N_HEADS
DECODE_SLOTS = 3


def _decode_scratch(pages_per_chunk):
    chunk_tokens = pages_per_chunk * PAGE
    return [pltpu.VMEM((DECODE_SLOTS, D_QK, chunk_tokens), F32),
            pltpu.VMEM((DECODE_SLOTS, chunk_tokens * N_HEADS, V_DIM), F32),
            pltpu.SemaphoreType.DMA((2, DECODE_SLOTS)),
            pltpu.VMEM((N_MAPS, 1), F32), pltpu.VMEM((N_MAPS, 1), F32), pltpu.VMEM((N_MAPS, D_ATT), F32)]


def _make_decode_stream(pt_ref, lamp_ref, sg_ref, q_ref, kn_ref, vn_ref, ck_hbm, cv_hbm, o_ref,
                        kbuf, vbuf, sem, m_ref, l_ref, acc_ref, *, tile, n_tiles, per_tile, n_chunks,
                        pages_per_chunk, n_pages):
    assert per_tile % n_chunks == 0 and DECODE_SLOTS - 1 <= per_tile
    seqs_per_tile = per_tile // n_chunks
    chunk_tokens = pages_per_chunk * PAGE
    past_len = n_pages * PAGE
    row = lax.broadcasted_iota(jnp.int32, (N_MAPS, D_QK), 0)
    lane = lax.broadcasted_iota(jnp.int32, (N_MAPS, D_QK), 1)
    row1 = lax.broadcasted_iota(jnp.int32, (N_MAPS, 1), 0)
    slope = jnp.exp2(-8.0 * ((row1 // 2) + 1).astype(F32) / N_HEADS)
    heads = [slice(h * V_DIM, (h + 1) * V_DIM) for h in range(N_HEADS)]

    def coords(c):
        t, c = tile + c // per_tile, c % per_tile
        return t * seqs_per_tile + c // n_chunks, c % n_chunks, (t * per_tile + c) % DECODE_SLOTS

    def page_copies(seq, chunk, slot):
        copies = []
        for p in range(pages_per_chunk):
            page = pt_ref[seq * n_pages + chunk * pages_per_chunk + p]
            copies.append(pltpu.make_async_copy(
                ck_hbm.at[page], kbuf.at[slot, :, pl.ds(p * PAGE, PAGE)], sem.at[0, slot]))
            copies.append(pltpu.make_async_copy(
                cv_hbm.at[page], vbuf.at[slot, pl.ds(p * PAGE * N_HEADS, PAGE * N_HEADS), :], sem.at[1, slot]))
        return copies

    def start(copies):
        for cp in copies:
            cp.start()

    def decode_chunk(c):
        seq, chunk, slot = coords(c)
        ahead = DECODE_SLOTS - 1
        if c == 0:
            @pl.when(tile == 0)
            def _():
                for first_chunks in range(ahead):
                    start(page_copies(*coords(first_chunks)))
        if c + ahead < per_tile:
            start(page_copies(*coords(c + ahead)))
        else:
            @pl.when(tile + 1 < n_tiles)
            def _():
                start(page_copies(*coords(c + ahead)))
        for cp in page_copies(seq, chunk, slot):
            cp.wait()

        q = q_ref[pl.ds(seq, 1), :]
        qblk = jnp.where(lane // HEAD_DIM == row, jnp.broadcast_to(q, (N_MAPS, D_QK)), 0.0)
        kt = kbuf[slot].astype(BF16)
        s = jnp.dot(qblk.astype(BF16), kt, preferred_element_type=F32)
        pos = chunk * chunk_tokens + lax.broadcasted_iota(jnp.int32, (1, chunk_tokens), 1)
        s = s - slope * (past_len - pos).astype(F32)
        s_max = jnp.max(s, axis=-1, keepdims=True)
        first, last = chunk == 0, chunk == n_chunks - 1
        m_new = s_max if first else jnp.maximum(m_ref[...], s_max)
        p = jnp.exp(s - m_new)
        l_new = jnp.sum(p, axis=-1, keepdims=True)
        if not first:
            alpha = jnp.exp(m_ref[...] - m_new)
            l_new = alpha * l_ref[...] + l_new
        pb = p.astype(BF16)
        acc = []
        for h, hs in enumerate(heads):
            vh = vbuf[slot, pl.ds(h, chunk_tokens, stride=N_HEADS), :].astype(BF16)
            pv = jnp.dot(pb, vh, preferred_element_type=F32)
            acc.append(pv if first else alpha * acc_ref[:, hs] + pv)
        if not last:
            m_ref[...] = m_new
            l_ref[...] = l_new
            for hs, a in zip(heads, acc):
                acc_ref[:, hs] = a
            return
        s_new = jnp.sum(qblk * kn_ref[pl.ds(seq, 1), :], axis=-1, keepdims=True)
        m_fin = jnp.maximum(m_new, s_new)
        alpha = jnp.exp(m_new - m_fin)
        p_new = jnp.exp(s_new - m_fin)
        l_fin = alpha * l_new + p_new
        acc = alpha * jnp.concatenate(acc, axis=-1) + p_new * vn_ref[pl.ds(seq, 1), :]
        lam = _diff_lambda(lamp_ref)
        coef = jnp.where(row1 % 2 == 0, 1.0, -lam) / l_fin
        own = lane // V_DIM == row // 2
        o = jnp.sum(jnp.where(own, coef * acc, 0.0), axis=0, keepdims=True)
        o_ref[pl.ds(seq, 1), :] = jnp.concatenate(
            [_rms(o[:, hs], sg_ref[...]) * (1.0 - LAM_INIT) for hs in heads], axis=-1)

    return decode_chunk


def _ssm_body(u_ref, h0_ref, ar_ref, ai_ref, bre_ref, bim_ref, cre_ref, cim_ref, d_ref, wg_ref, bg_ref,
              s_ref, h_ref, x_scr, *maybe_regroup_scr, rows_per_step, n_steps):
    r = rows_per_step
    slabs = [slice(j * LANES, (j + 1) * LANES) for j in range(D_SSM // LANES)]

    @pl.when(pl.program_id(0) == 0)
    def _():
        h_ref[...] = h0_ref[...]

    if maybe_regroup_scr:
        regroup_scr, = maybe_regroup_scr
        for b in range(r):
            for j, sl in enumerate(slabs):
                regroup_scr[j, pl.ds(b, n_steps, stride=r), :] = u_ref[b, :, sl]
        u = jnp.concatenate([regroup_scr[j] for j in range(len(slabs))], axis=-1)
    else:
        u = u_ref[...]
    ub = u.astype(BF16)
    for n in range(N_SSM_CHUNKS):
        un = ub[:, n * MXU_COLS:(n + 1) * MXU_COLS]
        cols = slice(n * SSM_CHUNK_STATES, (n + 1) * SSM_CHUNK_STATES)
        x_scr[:, cols] = jnp.dot(un, bre_ref[n], preferred_element_type=F32)
        x_scr[:, N_STATE + n * SSM_CHUNK_STATES:N_STATE + (n + 1) * SSM_CHUNK_STATES] = jnp.dot(
            un, bim_ref[n], preferred_element_type=F32)

    for cb in range(N_STATE // SCAN_COLS):
        re = slice(cb * SCAN_COLS, (cb + 1) * SCAN_COLS)
        im = slice(N_STATE + cb * SCAN_COLS, N_STATE + (cb + 1) * SCAN_COLS)
        a_r = jnp.broadcast_to(ar_ref[:, re], (r, SCAN_COLS))
        a_i = jnp.broadcast_to(ai_ref[:, re], (r, SCAN_COLS))

        def step(t, carry, re=re, im=im, a_r=a_r, a_i=a_i):
            hr, hi = carry
            rows = pl.ds(pl.multiple_of(t * r, r), r)
            nhr = a_r * hr - a_i * hi + x_scr[rows, re]
            nhi = a_r * hi + a_i * hr + x_scr[rows, im]
            x_scr[rows, re] = nhr
            x_scr[rows, im] = nhi
            return nhr, nhi

        carry = (h_ref[:, re], h_ref[:, im])
        if n_steps == 1:
            hr, hi = step(0, carry)
        else:
            hr, hi = lax.fori_loop(0, n_steps, step, carry, unroll=8)
        h_ref[:, re] = hr
        h_ref[:, im] = hi

    ys = []
    for n in range(N_SSM_CHUNKS):
        cols = slice(n * SSM_CHUNK_STATES, (n + 1) * SSM_CHUNK_STATES)
        cols_im = slice(N_STATE + n * SSM_CHUNK_STATES, N_STATE + (n + 1) * SSM_CHUNK_STATES)
        ys.append(jnp.dot(x_scr[:, cols].astype(BF16), cre_ref[n], preferred_element_type=F32)
                  - jnp.dot(x_scr[:, cols_im].astype(BF16), cim_ref[n], preferred_element_type=F32))
    y = jnp.concatenate(ys, axis=-1) + d_ref[...] * u
    g = 0.5 * y * (1.0 + jnp.tanh(math.sqrt(2.0 / math.pi) * (y + 0.044715 * (y * y * y))))
    z = jnp.dot(g.astype(BF16), wg_ref[...], preferred_element_type=F32) + bg_ref[...]
    s = g * _sigmoid(z)
    if maybe_regroup_scr:
        for j, sl in enumerate(slabs):
            regroup_scr[j] = s[:, sl]
        for b in range(r):
            s_ref[b] = jnp.concatenate([regroup_scr[j, pl.ds(b, n_steps, stride=r), :]
                                        for j in range(len(slabs))], axis=-1).astype(s_ref.dtype)
    else:
        s_ref[...] = s.astype(s_ref.dtype)


def _ssm_glu(u, h0, a_re, a_im, b_re_bd, b_im_bd, c_re_bd, c_im_bd, d_skip, w_glu_bf, b_glu,
             *, steps_per_block):
    rows_per_step = u.shape[0]
    blk = rows_per_step * steps_per_block
    body = functools.partial(_ssm_body, rows_per_step=rows_per_step, n_steps=steps_per_block)
    if u.ndim == 3:
        n_blocks = u.shape[1] // steps_per_block
        io_spec = pl.BlockSpec((rows_per_step, steps_per_block, D_SSM), lambda i: (0, i, 0))
        scratch = [pltpu.VMEM((D_SSM // LANES, blk, LANES), F32)]
    else:
        assert steps_per_block == 1
        n_blocks = 1
        io_spec = pl.BlockSpec((rows_per_step, D_SSM), lambda i: (0, 0))
        scratch = []
    return pl.pallas_call(
        body,
        grid=(n_blocks,),
        in_specs=[io_spec,
                  _resident((rows_per_step, 2 * N_STATE)),
                  _resident((1, N_STATE)), _resident((1, N_STATE)),
                  _resident((N_SSM_CHUNKS, MXU_COLS, SSM_CHUNK_STATES)),
                  _resident((N_SSM_CHUNKS, MXU_COLS, SSM_CHUNK_STATES)),
                  _resident((N_SSM_CHUNKS, SSM_CHUNK_STATES, MXU_COLS)),
                  _resident((N_SSM_CHUNKS, SSM_CHUNK_STATES, MXU_COLS)),
                  _resident((1, D_SSM)), _resident((D_SSM, D_SSM)), _resident((1, D_SSM))],
        out_specs=[io_spec, pl.BlockSpec((rows_per_step, 2 * N_STATE), lambda i: (0, 0))],
        out_shape=[jax.ShapeDtypeStruct(u.shape, BF16),
                   jax.ShapeDtypeStruct((rows_per_step, 2 * N_STATE), F32)],
        scratch_shapes=[pltpu.VMEM((blk, 2 * N_STATE), F32)] + scratch,
        compiler_params=_params(1), name="ssm_glu")(
            u, h0, a_re, a_im, b_re_bd, b_im_bd, c_re_bd, c_im_bd, d_skip, w_glu_bf, b_glu)


def _out_ffn_body(x_ref, o_ref, s_ref, cprev_ref, wo_ref, g2_ref, wup_ref, cw_ref, cb_ref, wdn_ref, gf_ref,
                  y_ref, cout_ref, acc_ref, *, rows_are_positions, before_chunk=None):
    tm = x_ref.shape[0]
    x1 = (x_ref[...]
          + jnp.dot(o_ref[...], wo_ref[:D_ATT, :], preferred_element_type=F32)
          + jnp.dot(s_ref[...], wo_ref[D_ATT:, :], preferred_element_type=F32))
    xn = _rms(x1, g2_ref[...]).astype(BF16)

    if rows_are_positions:
        @pl.when(pl.program_id(1) == 0)
        def _():
            cout_ref[...] = cprev_ref[...]
        hist = cout_ref[...]
        rid = lax.broadcasted_iota(jnp.int32, (tm, FF_CHUNK), 0)
    else:
        hist = cprev_ref[...]

    acc_ref[...] = jnp.zeros(acc_ref.shape, F32)
    for c in range(N_FF_CHUNKS):
        if before_chunk is not None:
            before_chunk(c)
        cs = slice(c * FF_CHUNK, (c + 1) * FF_CHUNK)
        cs_up = slice(D_FF + c * FF_CHUNK, D_FF + (c + 1) * FF_CHUNK)
        gate = jnp.dot(xn, wup_ref[:, cs], preferred_element_type=F32)
        up = jnp.dot(xn, wup_ref[:, cs_up], preferred_element_type=F32)
        if rows_are_positions:
            p0, p1 = hist[0:1, cs], hist[1:2, cs]
            g1 = jnp.where(rid == 0, p1, pltpu.roll(gate, 1, 0))
            g2 = jnp.where(rid == 0, p0, jnp.where(rid == 1, p1, pltpu.roll(gate, 2, 0)))
            cout_ref[:, cs] = gate[tm - 2:, :]
        else:
            g2, g1 = hist[:, cs], hist[:, cs_up]
            cout_ref[:, cs] = g1
            cout_ref[:, cs_up] = gate
        gc = cb_ref[:, cs] + cw_ref[0:1, cs] * g2 + cw_ref[1:2, cs] * g1 + cw_ref[2:3, cs] * gate
        act = (gc * _sigmoid(gc) * up).astype(BF16)
        acc_ref[...] += jnp.dot(act, wdn_ref[cs, :], preferred_element_type=F32)
    y_ref[...] = _rms(x1 + acc_ref[...], gf_ref[...])


def _out_ffn_specs(x, conv_prev, tm):
    conv_spec = pl.BlockSpec((None,) + conv_prev.shape[1:], lambda i, j, *_: (i, 0, 0))
    tok_spec = lambda width: pl.BlockSpec((None, tm, width), lambda i, j, *_: (i, j, 0))
    in_specs = [tok_spec(D_MODEL), tok_spec(D_ATT), tok_spec(D_SSM), conv_spec,
                _resident((D_MODEL, D_MODEL)), _resident((1, D_MODEL)),
                _resident((D_MODEL, 2 * D_FF)), _resident((3, D_FF)), _resident((1, D_FF)),
                _resident((D_FF, D_MODEL)), _resident((1, D_MODEL))]
    out_specs = [tok_spec(D_MODEL), conv_spec]
    out_shape = [jax.ShapeDtypeStruct(x.shape, F32), jax.ShapeDtypeStruct(conv_prev.shape, F32)]
    return in_specs, out_specs, out_shape, [pltpu.VMEM((tm, D_MODEL), F32)]


def _out_ffn(x, o, s, conv_prev, *ffn_w, tm, rows_are_positions):
    in_specs, out_specs, out_shape, scratch = _out_ffn_specs(x, conv_prev, tm)
    return pl.pallas_call(
        functools.partial(_out_ffn_body, rows_are_positions=rows_are_positions),
        grid=(x.shape[0], x.shape[1] // tm),
        in_specs=in_specs, out_specs=out_specs, out_shape=out_shape, scratch_shapes=scratch,
        compiler_params=_params(2), name="out_ffn")(x, o, s, conv_prev, *ffn_w)


N_FFN_REFS = 11


def _out_ffn_decode_body(pt_ref, *refs, per_tile, n_chunks, pages_per_chunk, n_pages):
    ffn_in, refs = refs[:N_FFN_REFS], refs[N_FFN_REFS:]
    (lamp_ref, sg_ref, q_ref, kn_ref, vn_ref, ck_hbm, cv_hbm), refs = refs[:7], refs[7:]
    (y_ref, cout_ref, od_ref, acc_ref), dec_scratch = refs[:4], refs[4:]
    tile = pl.program_id(0) * pl.num_programs(1) + pl.program_id(1)
    decode_chunk = _make_decode_stream(
        pt_ref, lamp_ref, sg_ref, q_ref, kn_ref, vn_ref, ck_hbm, cv_hbm, od_ref, *dec_scratch,
        tile=tile, n_tiles=pl.num_programs(0) * pl.num_programs(1), per_tile=per_tile, n_chunks=n_chunks,
        pages_per_chunk=pages_per_chunk, n_pages=n_pages)

    def before_chunk(c):
        for d in range(per_tile):
            if (d * N_FF_CHUNKS) // per_tile == c:
                decode_chunk(d)

    _out_ffn_body(*ffn_in, y_ref, cout_ref, acc_ref, rows_are_positions=True, before_chunk=before_chunk)


def _out_ffn_with_decode(x, o, s, conv_prev, ffn_w, page_table, lamp, subln_g, q, k_new, v_new,
                         cache_k, cache_v, *, tm, pages_per_chunk):
    n_tiles = x.shape[0] * (x.shape[1] // tm)
    n_dec, n_pages = page_table.shape
    n_chunks = n_pages // pages_per_chunk
    per_tile, rem = divmod(n_dec * n_chunks, n_tiles)
    assert rem == 0
    in_specs, out_specs, out_shape, scratch = _out_ffn_specs(x, conv_prev, tm)
    seq_rows = _resident((n_dec, D_QK))
    hbm = pl.BlockSpec(memory_space=pl.ANY)
    grid_spec = pltpu.PrefetchScalarGridSpec(
        num_scalar_prefetch=1,
        grid=(x.shape[0], x.shape[1] // tm),
        in_specs=in_specs + [_resident((4, HEAD_DIM)), _resident((1, V_DIM)), seq_rows, seq_rows, seq_rows, hbm, hbm],
        out_specs=out_specs + [pl.BlockSpec((n_dec, D_ATT), lambda *_: (0, 0))],
        scratch_shapes=scratch + _decode_scratch(pages_per_chunk))
    body = functools.partial(_out_ffn_decode_body, per_tile=per_tile, n_chunks=n_chunks,
                             pages_per_chunk=pages_per_chunk, n_pages=n_pages)
    return pl.pallas_call(
        body, grid_spec=grid_spec, out_shape=out_shape + [jax.ShapeDtypeStruct((n_dec, D_ATT), F32)],
        compiler_params=_params(2), name="out_ffn_decode")(
            page_table.reshape(-1), x, o, s, conv_prev, *ffn_w, lamp, subln_g, q, k_new, v_new, cache_k, cache_v)


def kernel(x_prompt, x_sample, cache_k, cache_v, state_ssm_re, state_ssm_im, cache_ffn_conv, page_table,
           norm1_g, w_in, lam_q1, lam_k1, lam_q2, lam_k2, subln_g, ssm_lam_re, ssm_lam_im, ssm_log_dt,
           ssm_b_re, ssm_b_im, ssm_c_re, ssm_c_im, ssm_d, w_glu, b_glu, w_out, norm2_g, w_up, conv_w,
           conv_b, w_down, final_g):
    assert w_in.shape[0] == 1, "single-layer step"
    n_batch, seq, _ = x_prompt.shape
    n_dec = x_sample.shape[0]
    n_pool = cache_k.shape[1]

    w_in_bf, w_out_bf = w_in[0].astype(BF16), w_out[0].astype(BF16)
    w_up_bf, w_down_bf, w_glu_bf = w_up[0].astype(BF16), w_down[0].astype(BF16), w_glu[0].astype(BF16)
    g1, g2, gf = norm1_g[0].reshape(1, -1), norm2_g[0].reshape(1, -1), final_g.reshape(1, -1)
    lamp = jnp.stack([lam_q1[0], lam_k1[0], lam_q2[0], lam_k2[0]])
    sg = subln_g[0].reshape(1, V_DIM)
    a_re, a_im, bb_re, bb_im = _ssm_discretize(ssm_lam_re[0], ssm_lam_im[0], ssm_log_dt[0],
                                               ssm_b_re[0], ssm_b_im[0])
    ssm_w = (a_re, a_im, _blockdiag_in(bb_re).astype(BF16), _blockdiag_in(bb_im).astype(BF16),
             _blockdiag_out(ssm_c_re[0]).astype(BF16), _blockdiag_out(ssm_c_im[0]).astype(BF16),
             ssm_d[0].reshape(1, -1), w_glu_bf, b_glu[0].reshape(1, -1))
    ffn_w = (w_out_bf, g2, w_up_bf, conv_w[0], conv_b[0].reshape(1, -1), w_down_bf, gf)

    q, k, v, u, k_bf, v_bf = _in_proj(x_prompt, g1, w_in_bf, tm=512, q_dtype=BF16, bf16_kv=True)
    o = _prompt_attention(lamp, sg, q, k_bf, v_bf, tq=256, tk=512)
    s, h_p = _ssm_glu(u, jnp.zeros((n_batch, 2 * N_STATE), F32), *ssm_w, steps_per_block=64)

    xs = x_sample.reshape(1, n_dec, D_MODEL)
    qs, ks, vs, us = _in_proj(xs, g1, w_in_bf, tm=n_dec, q_dtype=F32, bf16_kv=False)
    seq_rows = lambda t: t.reshape(n_dec, D_QK)
    y_p, conv_p, o_s = _out_ffn_with_decode(
        x_prompt, o, s, jnp.zeros((n_batch, 2, D_FF), F32), ffn_w,
        page_table, lamp, sg, seq_rows(qs), seq_rows(ks), seq_rows(vs),
        cache_k.transpose(0, 1, 3, 4, 5, 2).reshape(n_pool, D_QK, PAGE),
        cache_v.reshape(n_pool, PAGE * N_HEADS, V_DIM),
        tm=256, pages_per_chunk=16)
    h0_s = jnp.concatenate([state_ssm_re[0].reshape(n_dec, N_STATE),
                            state_ssm_im[0].reshape(n_dec, N_STATE)], axis=-1)
    s_s, h_s = _ssm_glu(us.reshape(n_dec, D_SSM), h0_s, *ssm_w, steps_per_block=1)
    y_s, conv_s = _out_ffn(xs, o_s.reshape(1, n_dec, D_ATT).astype(BF16), s_s.reshape(1, n_dec, D_SSM),
                           cache_ffn_conv[0].reshape(1, n_dec, 2 * D_FF), *ffn_w,
                           tm=n_dec, rows_are_positions=False)

    heads_k = lambda t, lead: t.reshape(1, *lead, N_HEADS, 2, HEAD_DIM)
    heads_v = lambda t, lead: t.reshape(1, *lead, N_HEADS, V_DIM)
    state = lambda h, n: h.reshape(1, n, N_GROUPS, SSM_STATE)
    return (y_p, y_s.reshape(n_dec, 1, D_MODEL),
            heads_k(k, (n_batch, seq)), heads_v(v, (n_batch, seq)),
            heads_k(ks, (n_dec, 1)), heads_v(vs, (n_dec, 1)),
            state(h_p[:, :N_STATE], n_batch), state(h_p[:, N_STATE:], n_batch),
            state(h_s[:, :N_STATE], n_dec), state(h_s[:, N_STATE:], n_dec),
            conv_p.reshape(1, n_batch, 2, D_FF), conv_s.reshape(1, n_dec, 2, D_FF))
```

```python
import functools
import math

import jax
import jax.numpy as jnp
from jax import lax
from jax.experimental import pallas as pl
from jax.experimental.pallas import tpu as pltpu

F32 = jnp.float32
BF16 = jnp.bfloat16

D_MODEL = 1024
N_HEADS = 4
HEAD_DIM = 64
V_DIM = 2 * HEAD_DIM
D_ATT = N_HEADS * V_DIM
D_QK = N_HEADS * 2 * HEAD_DIM
D_SSM = 512
SSM_GROUP = 16
N_GROUPS = D_SSM // SSM_GROUP
SSM_STATE = 64
N_STATE = N_GROUPS * SSM_STATE
D_IN = 2 * D_QK + D_ATT + D_SSM
D_FF = 2816
PAGE = 128
EPS = 1e-6
LAM_INIT = 0.8 - 0.6 * math.exp(-0.3 * 0)
QK_SCALE = HEAD_DIM ** -0.5

LANES = 128
SUBLANES = 8
MXU_COLS = 256
VMEM_LIMIT = 56 * 1024 * 1024

FF_CHUNK = MXU_COLS
N_FF_CHUNKS = D_FF // FF_CHUNK
SSM_CHUNK_GROUPS = MXU_COLS // SSM_GROUP
SSM_CHUNK_STATES = SSM_CHUNK_GROUPS * SSM_STATE
N_SSM_CHUNKS = D_SSM // MXU_COLS
SCAN_COLS = 512


def _params(n_axes, vmem=VMEM_LIMIT):
    return pltpu.CompilerParams(dimension_semantics=("arbitrary",) * n_axes, vmem_limit_bytes=vmem)


def _rms(x, g):
    return x * lax.rsqrt(jnp.mean(x * x, axis=-1, keepdims=True) + EPS) * g


def _sigmoid(x):
    return 1.0 / (1.0 + jnp.exp(-x))


def _diff_lambda(lamp_ref):
    lp = lamp_ref[...]
    a1 = jnp.sum(lp[0:1] * lp[1:2], axis=-1, keepdims=True)
    a2 = jnp.sum(lp[2:3] * lp[3:4], axis=-1, keepdims=True)
    return jnp.exp(a1) - jnp.exp(a2) + LAM_INIT


def _resident(shape):
    return pl.BlockSpec(shape, lambda *_: (0,) * len(shape), pipeline_mode=pl.Buffered(1))


def _disc_body(lr_ref, li_ref, ldt_ref, br_ref, bi_ref, ar_ref, ai_ref, bbr_ref, bbi_ref):
    lr, li = lr_ref[...], li_ref[...]
    dt = jnp.exp(ldt_ref[...])
    mag = jnp.exp(lr * dt)
    a_re, a_im = mag * jnp.cos(li * dt), mag * jnp.sin(li * dt)
    den = lr * lr + li * li
    nr, ni = a_re - 1.0, a_im
    coef_re = (nr * lr + ni * li) / den
    coef_im = (ni * lr - nr * li) / den
    br, bi = br_ref[...], bi_ref[...]
    bbr_ref[...] = coef_re * br - coef_im * bi
    bbi_ref[...] = coef_re * bi + coef_im * br
    ar_ref[...] = a_re
    ai_ref[...] = a_im


def _ssm_discretize(lam_re, lam_im, log_dt, b_re, b_im):
    row = jax.ShapeDtypeStruct((1, N_STATE), F32)
    mat = jax.ShapeDtypeStruct((SSM_GROUP, N_STATE), F32)
    return pl.pallas_call(_disc_body, out_shape=(row, row, mat, mat), name="ssm_discretize")(
        lam_re.reshape(1, N_STATE), lam_im.reshape(1, N_STATE),
        jnp.repeat(log_dt, SSM_STATE).reshape(1, N_STATE),
        b_re.transpose(2, 0, 1).reshape(SSM_GROUP, N_STATE),
        b_im.transpose(2, 0, 1).reshape(SSM_GROUP, N_STATE))


def _blockdiag_in(bbt):
    t = bbt.reshape(SSM_GROUP, N_SSM_CHUNKS, SSM_CHUNK_GROUPS, SSM_STATE)
    eye = jnp.eye(SSM_CHUNK_GROUPS, dtype=bbt.dtype)
    return jnp.einsum('cngp,gh->ngchp', t, eye).reshape(N_SSM_CHUNKS, MXU_COLS, SSM_CHUNK_STATES)


def _blockdiag_out(c):
    t = c.reshape(N_SSM_CHUNKS, SSM_CHUNK_GROUPS, SSM_GROUP, SSM_STATE)
    eye = jnp.eye(SSM_CHUNK_GROUPS, dtype=c.dtype)
    return jnp.einsum('ngkp,gh->ngphk', t, eye).reshape(N_SSM_CHUNKS, SSM_CHUNK_STATES, MXU_COLS)


def _inproj_body(x_ref, g_ref, w_ref, q_ref, k_ref, v_ref, u_ref, *maybe_bf16_kv):
    xn = _rms(x_ref[...], g_ref[...]).astype(BF16)

    def proj(lo):
        return jnp.dot(xn, w_ref[:, lo:lo + D_QK], preferred_element_type=F32)

    q_ref[...] = (proj(0) * QK_SCALE).astype(q_ref.dtype)
    k = proj(D_QK)
    v = proj(2 * D_QK)
    k_ref[...] = k
    for h in range(N_HEADS):
        v_ref[pl.ds(h, x_ref.shape[0], stride=N_HEADS), :] = v[:, h * V_DIM:(h + 1) * V_DIM]
    u_ref[...] = proj(2 * D_QK + D_ATT)
    if maybe_bf16_kv:
        kb_ref, vb_ref = maybe_bf16_kv
        kb_ref[...] = k.astype(BF16)
        vb_ref[...] = v.astype(BF16)


def _in_proj(x, g, w_bf, *, tm, q_dtype, bf16_kv):
    b, s, _ = x.shape
    tok = lambda dt: jax.ShapeDtypeStruct((b, s, D_QK), dt)
    tok_spec = pl.BlockSpec((None, tm, D_QK), lambda i, j: (i, j, 0))
    out_shape = [tok(q_dtype), tok(F32), jax.ShapeDtypeStruct((b, s * N_HEADS, V_DIM), F32), tok(F32)]
    out_specs = [tok_spec, tok_spec, pl.BlockSpec((None, tm * N_HEADS, V_DIM), lambda i, j: (i, j, 0)), tok_spec]
    if bf16_kv:
        out_shape += [tok(BF16), tok(BF16)]
        out_specs += [tok_spec, tok_spec]
    return pl.pallas_call(
        _inproj_body,
        grid=(b, s // tm),
        in_specs=[pl.BlockSpec((None, tm, D_MODEL), lambda i, j: (i, j, 0)),
                  _resident((1, D_MODEL)), _resident((D_MODEL, D_IN))],
        out_specs=out_specs, out_shape=out_shape,
        compiler_params=_params(2), name="in_proj")(x, g, w_bf)


POS_RADIX = 256


def _alibi_key_features(seq):
    j = lax.broadcasted_iota(jnp.int32, (seq, LANES), 0)
    lane = lax.broadcasted_iota(jnp.int32, (seq, LANES), 1)
    feat = jnp.where(lane == 0, j // POS_RADIX, jnp.where(lane == 1, j % POS_RADIX, 0))
    return feat.astype(BF16)


def _prompt_attn_body(lamp_ref, sg_ref, q_ref, k_ref, v_ref, pos_ref, o_ref, m_scr, acc_scr, qs_scr, s_scr,
                      *, tq, tk):
    qi = pl.program_id(1)
    lam = _diff_lambda(lamp_ref)
    lane = lax.broadcasted_iota(jnp.int32, (1, LANES), 1)
    n_full = (qi * tq) // tk
    diag_off = qi * tq - n_full * tk
    row_in_tile = lax.broadcasted_iota(jnp.int32, (2 * tq, tk), 0) % tq
    col = lax.broadcasted_iota(jnp.int32, (2 * tq, tk), 1)
    causal = row_in_tile + diag_off >= col
    ones_col = jnp.broadcast_to(jnp.where(lane == 0, 1.0, 0.0).astype(BF16), (tk, LANES))
    nt = (((1,), (1,)), ((), ()))

    heads = [slice(h * V_DIM, (h + 1) * V_DIM) for h in range(N_HEADS)]
    for h, hs in enumerate(heads):
        slope = 2.0 ** (-8.0 * (h + 1) / N_HEADS)
        q = q_ref[:, hs]
        zero = jnp.zeros_like(q)
        q_pos = jnp.broadcast_to(
            jnp.where(lane == 0, POS_RADIX * slope, jnp.where(lane == 1, slope, 0.0)).astype(BF16), (tq, LANES))
        qs_scr[h, :tq, :LANES] = jnp.where(lane < HEAD_DIM, q, zero)
        qs_scr[h, tq:, :LANES] = jnp.where(lane >= HEAD_DIM, q, zero)
        qs_scr[h, :tq, LANES:] = q_pos
        qs_scr[h, tq:, LANES:] = q_pos
    m_scr[...] = jnp.full(m_scr.shape, -jnp.inf, F32)
    acc_scr[...] = jnp.zeros(acc_scr.shape, F32)

    def kv_rows(j):
        return pl.ds(pl.multiple_of(j * tk, tk), tk)

    def scores(j):
        rows = kv_rows(j)
        pos = pos_ref[rows, :]
        for h, hs in enumerate(heads):
            kx = jnp.concatenate([k_ref[rows, hs], pos], axis=1)
            s_scr[j % 2, h] = lax.dot_general(qs_scr[h], kx, nt, preferred_element_type=F32)

    def absorb(j, masked):
        rows = kv_rows(j)
        for h, hs in enumerate(heads):
            vx = jnp.concatenate([v_ref[rows, hs], ones_col], axis=1)
            s = s_scr[j % 2, h]
            if masked:
                s = jnp.where(causal, s, -jnp.inf)
            m_old = m_scr[h]
            m_new = jnp.maximum(m_old, jnp.max(s, axis=-1, keepdims=True))
            alpha = jnp.exp(m_old - m_new)
            p = jnp.exp(s - jnp.concatenate([m_new] * (tk // LANES), axis=1)).astype(BF16)
            acc_scr[h] = (jnp.concatenate([alpha, alpha], axis=1) * acc_scr[h]
                          + jnp.dot(p, vx, preferred_element_type=F32))
            m_scr[h] = m_new

    def pipelined(j, carry):
        absorb(j, False)
        scores(j + 1)
        return carry

    scores(0)
    lax.fori_loop(0, n_full, pipelined, 0)
    absorb(n_full, True)
    for h, hs in enumerate(heads):
        acc = acc_scr[h]
        a1, a2 = acc[:tq], acc[tq:]
        o = (a1[:, :V_DIM] / a1[:, V_DIM:V_DIM + 1]
             - lam * (a2[:, :V_DIM] / a2[:, V_DIM:V_DIM + 1]))
        o_ref[:, hs] = (_rms(o, sg_ref[...]) * (1.0 - LAM_INIT)).astype(o_ref.dtype)


def _prompt_attention(lamp, subln_g, q, k, v, *, tq, tk):
    b, s, _ = q.shape
    assert s % tk == 0 and tk % tq == 0
    seq_spec = pl.BlockSpec((None, s, D_QK), lambda i, j: (i, 0, 0))
    return pl.pallas_call(
        functools.partial(_prompt_attn_body, tq=tq, tk=tk),
        grid=(b, s // tq),
        in_specs=[_resident((4, HEAD_DIM)), _resident((1, V_DIM)),
                  pl.BlockSpec((None, tq, D_QK), lambda i, j: (i, j, 0)), seq_spec, seq_spec,
                  _resident((s, LANES))],
        out_specs=pl.BlockSpec((None, tq, D_ATT), lambda i, j: (i, j, 0)),
        out_shape=jax.ShapeDtypeStruct((b, s, D_ATT), BF16),
        scratch_shapes=[pltpu.VMEM((N_HEADS, 2 * tq, LANES), F32), pltpu.VMEM((N_HEADS, 2 * tq, 2 * LANES), F32),
                        pltpu.VMEM((N_HEADS, 2 * tq, 2 * LANES), BF16),
                        pltpu.VMEM((2, N_HEADS, 2 * tq, tk), F32)],
        compiler_params=_params(2), name="prompt_attn")(lamp, subln_g, q, k, v, _alibi_key_features(s))


N_MAPS = 2 * N_HEADS
DECODE_SLOTS = 3


def _decode_scratch(pages_per_chunk):
    chunk_tokens = pages_per_chunk * PAGE
    return [pltpu.VMEM((DECODE_SLOTS, D_QK, chunk_tokens), F32),
            pltpu.VMEM((DECODE_SLOTS, chunk_tokens * N_HEADS, V_DIM), F32),
            pltpu.SemaphoreType.DMA((2, DECODE_SLOTS)),
            pltpu.VMEM((N_MAPS, 1), F32), pltpu.VMEM((N_MAPS, 1), F32), pltpu.VMEM((N_MAPS, D_ATT), F32)]


def _make_decode_stream(pt_ref, lamp_ref, sg_ref, q_ref, kn_ref, vn_ref, ck_hbm, cv_hbm, o_ref,
                        kbuf, vbuf, sem, m_ref, l_ref, acc_ref, *, tile, n_tiles, per_tile, n_chunks,
                        pages_per_chunk, n_pages):
    assert per_tile % n_chunks == 0 and DECODE_SLOTS - 1 <= per_tile
    seqs_per_tile = per_tile // n_chunks
    chunk_tokens = pages_per_chunk * PAGE
    past_len = n_pages * PAGE
    row = lax.broadcasted_iota(jnp.int32, (N_MAPS, D_QK), 0)
    lane = lax.broadcasted_iota(jnp.int32, (N_MAPS, D_QK), 1)
    row1 = lax.broadcasted_iota(jnp.int32, (N_MAPS, 1), 0)
    slope = jnp.exp2(-8.0 * ((row1 // 2) + 1).astype(F32) / N_HEADS)
    heads = [slice(h * V_DIM, (h + 1) * V_DIM) for h in range(N_HEADS)]

    def coords(c):
        t, c = tile + c // per_tile, c % per_tile
        return t * seqs_per_tile + c // n_chunks, c % n_chunks, (t * per_tile + c) % DECODE_SLOTS

    def page_copies(seq, chunk, slot):
        copies = []
        for p in range(pages_per_chunk):
            page = pt_ref[seq * n_pages + chunk * pages_per_chunk + p]
            copies.append(pltpu.make_async_copy(
                ck_hbm.at[page], kbuf.at[slot, :, pl.ds(p * PAGE, PAGE)], sem.at[0, slot]))
            copies.append(pltpu.make_async_copy(
                cv_hbm.at[page], vbuf.at[slot, pl.ds(p * PAGE * N_HEADS, PAGE * N_HEADS), :], sem.at[1, slot]))
        return copies

    def start(copies):
        for cp in copies:
            cp.start()

    def decode_chunk(c):
        seq, chunk, slot = coords(c)
        ahead = DECODE_SLOTS - 1
        if c == 0:
            @pl.when(tile == 0)
            def _():
                for first_chunks in range(ahead):
                    start(page_copies(*coords(first_chunks)))
        if c + ahead < per_tile:
            start(page_copies(*coords(c + ahead)))
        else:
            @pl.when(tile + 1 < n_tiles)
            def _():
                start(page_copies(*coords(c + ahead)))
        for cp in page_copies(seq, chunk, slot):
            cp.wait()

        q = q_ref[pl.ds(seq, 1), :]
        qblk = jnp.where(lane // HEAD_DIM == row, jnp.broadcast_to(q, (N_MAPS, D_QK)), 0.0)
        kt = kbuf[slot].astype(BF16)
        s = jnp.dot(qblk.astype(BF16), kt, preferred_element_type=F32)
        pos = chunk * chunk_tokens + lax.broadcasted_iota(jnp.int32, (1, chunk_tokens), 1)
        s = s - slope * (past_len - pos).astype(F32)
        s_max = jnp.max(s, axis=-1, keepdims=True)
        first, last = chunk == 0, chunk == n_chunks - 1
        m_new = s_max if first else jnp.maximum(m_ref[...], s_max)
        p = jnp.exp(s - m_new)
        l_new = jnp.sum(p, axis=-1, keepdims=True)
        if not first:
            alpha = jnp.exp(m_ref[...] - m_new)
            l_new = alpha * l_ref[...] + l_new
        pb = p.astype(BF16)
        acc = []
        for h, hs in enumerate(heads):
            vh = vbuf[slot, pl.ds(h, chunk_tokens, stride=N_HEADS), :].astype(BF16)
            pv = jnp.dot(pb, vh, preferred_element_type=F32)
            acc.append(pv if first else alpha * acc_ref[:, hs] + pv)
        if not last:
            m_ref[...] = m_new
            l_ref[...] = l_new
            for hs, a in zip(heads, acc):
                acc_ref[:, hs] = a
            return
        s_new = jnp.sum(qblk * kn_ref[pl.ds(seq, 1), :], axis=-1, keepdims=True)
        m_fin = jnp.maximum(m_new, s_new)
        alpha = jnp.exp(m_new - m_fin)
        p_new = jnp.exp(s_new - m_fin)
        l_fin = alpha * l_new + p_new
        acc = alpha * jnp.concatenate(acc, axis=-1) + p_new * vn_ref[pl.ds(seq, 1), :]
        lam = _diff_lambda(lamp_ref)
        coef = jnp.where(row1 % 2 == 0, 1.0, -lam) / l_fin
        own = lane // V_DIM == row // 2
        o = jnp.sum(jnp.where(own, coef * acc, 0.0), axis=0, keepdims=True)
        o_ref[pl.ds(seq, 1), :] = jnp.concatenate(
            [_rms(o[:, hs], sg_ref[...]) * (1.0 - LAM_INIT) for hs in heads], axis=-1)

    return decode_chunk


def _ssm_body(u_ref, h0_ref, ar_ref, ai_ref, bre_ref, bim_ref, cre_ref, cim_ref, d_ref, wg_ref, bg_ref,
              s_ref, h_ref, x_scr, *maybe_regroup_scr, rows_per_step, n_steps):
    r = rows_per_step
    slabs = [slice(j * LANES, (j + 1) * LANES) for j in range(D_SSM // LANES)]

    @pl.when(pl.program_id(0) == 0)
    def _():
        h_ref[...] = h0_ref[...]

    if maybe_regroup_scr:
        regroup_scr, = maybe_regroup_scr
        for b in range(r):
            for j, sl in enumerate(slabs):
                regroup_scr[j, pl.ds(b, n_steps, stride=r), :] = u_ref[b, :, sl]
        u = jnp.concatenate([regroup_scr[j] for j in range(len(slabs))], axis=-1)
    else:
        u = u_ref[...]
    ub = u.astype(BF16)
    for n in range(N_SSM_CHUNKS):
        un = ub[:, n * MXU_COLS:(n + 1) * MXU_COLS]
        cols = slice(n * SSM_CHUNK_STATES, (n + 1) * SSM_CHUNK_STATES)
        x_scr[:, cols] = jnp.dot(un, bre_ref[n], preferred_element_type=F32)
        x_scr[:, N_STATE + n * SSM_CHUNK_STATES:N_STATE + (n + 1) * SSM_CHUNK_STATES] = jnp.dot(
            un, bim_ref[n], preferred_element_type=F32)

    for cb in range(N_STATE // SCAN_COLS):
        re = slice(cb * SCAN_COLS, (cb + 1) * SCAN_COLS)
        im = slice(N_STATE + cb * SCAN_COLS, N_STATE + (cb + 1) * SCAN_COLS)
        a_r = jnp.broadcast_to(ar_ref[:, re], (r, SCAN_COLS))
        a_i = jnp.broadcast_to(ai_ref[:, re], (r, SCAN_COLS))

        def step(t, carry, re=re, im=im, a_r=a_r, a_i=a_i):
            hr, hi = carry
            rows = pl.ds(pl.multiple_of(t * r, r), r)
            nhr = a_r * hr - a_i * hi + x_scr[rows, re]
            nhi = a_r * hi + a_i * hr + x_scr[rows, im]
            x_scr[rows, re] = nhr
            x_scr[rows, im] = nhi
            return nhr, nhi

        carry = (h_ref[:, re], h_ref[:, im])
        if n_steps == 1:
            hr, hi = step(0, carry)
        else:
            hr, hi = lax.fori_loop(0, n_steps, step, carry, unroll=True)
        h_ref[:, re] = hr
        h_ref[:, im] = hi

    ys = []
    for n in range(N_SSM_CHUNKS):
        cols = slice(n * SSM_CHUNK_STATES, (n + 1) * SSM_CHUNK_STATES)
        cols_im = slice(N_STATE + n * SSM_CHUNK_STATES, N_STATE + (n + 1) * SSM_CHUNK_STATES)
        ys.append(jnp.dot(x_scr[:, cols].astype(BF16), cre_ref[n], preferred_element_type=F32)
                  - jnp.dot(x_scr[:, cols_im].astype(BF16), cim_ref[n], preferred_element_type=F32))
    y = jnp.concatenate(ys, axis=-1) + d_ref[...] * u
    g = 0.5 * y * (1.0 + jnp.tanh(math.sqrt(2.0 / math.pi) * (y + 0.044715 * (y * y * y))))
    z = jnp.dot(g.astype(BF16), wg_ref[...], preferred_element_type=F32) + bg_ref[...]
    s = g * _sigmoid(z)
    if maybe_regroup_scr:
        for j, sl in enumerate(slabs):
            regroup_scr[j] = s[:, sl]
        for b in range(r):
            s_ref[b] = jnp.concatenate([regroup_scr[j, pl.ds(b, n_steps, stride=r), :]
                                        for j in range(len(slabs))], axis=-1).astype(s_ref.dtype)
    else:
        s_ref[...] = s.astype(s_ref.dtype)


def _ssm_glu(u, h0, a_re, a_im, b_re_bd, b_im_bd, c_re_bd, c_im_bd, d_skip, w_glu_bf, b_glu,
             *, steps_per_block):
    rows_per_step = u.shape[0]
    blk = rows_per_step * steps_per_block
    body = functools.partial(_ssm_body, rows_per_step=rows_per_step, n_steps=steps_per_block)
    if u.ndim == 3:
        n_blocks = u.shape[1] // steps_per_block
        io_spec = pl.BlockSpec((rows_per_step, steps_per_block, D_SSM), lambda i: (0, i, 0))
        scratch = [pltpu.VMEM((D_SSM // LANES, blk, LANES), F32)]
    else:
        assert steps_per_block == 1
        n_blocks = 1
        io_spec = pl.BlockSpec((rows_per_step, D_SSM), lambda i: (0, 0))
        scratch = []
    return pl.pallas_call(
        body,
        grid=(n_blocks,),
        in_specs=[io_spec,
                  _resident((rows_per_step, 2 * N_STATE)),
                  _resident((1, N_STATE)), _resident((1, N_STATE)),
                  _resident((N_SSM_CHUNKS, MXU_COLS, SSM_CHUNK_STATES)),
                  _resident((N_SSM_CHUNKS, MXU_COLS, SSM_CHUNK_STATES)),
                  _resident((N_SSM_CHUNKS, SSM_CHUNK_STATES, MXU_COLS)),
                  _resident((N_SSM_CHUNKS, SSM_CHUNK_STATES, MXU_COLS)),
                  _resident((1, D_SSM)), _resident((D_SSM, D_SSM)), _resident((1, D_SSM))],
        out_specs=[io_spec, pl.BlockSpec((rows_per_step, 2 * N_STATE), lambda i: (0, 0))],
        out_shape=[jax.ShapeDtypeStruct(u.shape, BF16),
                   jax.ShapeDtypeStruct((rows_per_step, 2 * N_STATE), F32)],
        scratch_shapes=[pltpu.VMEM((blk, 2 * N_STATE), F32)] + scratch,
        compiler_params=_params(1), name="ssm_glu")(
            u, h0, a_re, a_im, b_re_bd, b_im_bd, c_re_bd, c_im_bd, d_skip, w_glu_bf, b_glu)


def _out_ffn_body(x_ref, o_ref, s_ref, cprev_ref, wo_ref, g2_ref, wup_ref, cw_ref, cb_ref, wdn_ref, gf_ref,
                  y_ref, cout_ref, acc_ref, *, rows_are_positions, before_chunk=None):
    tm = x_ref.shape[0]
    x1 = (x_ref[...]
          + jnp.dot(o_ref[...], wo_ref[:D_ATT, :], preferred_element_type=F32)
          + jnp.dot(s_ref[...], wo_ref[D_ATT:, :], preferred_element_type=F32))
    xn = _rms(x1, g2_ref[...]).astype(BF16)

    if rows_are_positions:
        @pl.when(pl.program_id(1) == 0)
        def _():
            cout_ref[...] = cprev_ref[...]
        hist = cout_ref[...]
        rid = lax.broadcasted_iota(jnp.int32, (tm, FF_CHUNK), 0)
    else:
        hist = cprev_ref[...]

    acc_ref[...] = jnp.zeros(acc_ref.shape, F32)
    for c in range(N_FF_CHUNKS):
        if before_chunk is not None:
            before_chunk(c)
        cs = slice(c * FF_CHUNK, (c + 1) * FF_CHUNK)
        cs_up = slice(D_FF + c * FF_CHUNK, D_FF + (c + 1) * FF_CHUNK)
        gate = jnp.dot(xn, wup_ref[:, cs], preferred_element_type=F32)
        up = jnp.dot(xn, wup_ref[:, cs_up], preferred_element_type=F32)
        if rows_are_positions:
            p0, p1 = hist[0:1, cs], hist[1:2, cs]
            g1 = jnp.where(rid == 0, p1, pltpu.roll(gate, 1, 0))
            g2 = jnp.where(rid == 0, p0, jnp.where(rid == 1, p1, pltpu.roll(gate, 2, 0)))
            cout_ref[:, cs] = gate[tm - 2:, :]
        else:
            g2, g1 = hist[:, cs], hist[:, cs_up]
            cout_ref[:, cs] = g1
            cout_ref[:, cs_up] = gate
        gc = cb_ref[:, cs] + cw_ref[0:1, cs] * g2 + cw_ref[1:2, cs] * g1 + cw_ref[2:3, cs] * gate
        act = (gc * _sigmoid(gc) * up).astype(BF16)
        acc_ref[...] += jnp.dot(act, wdn_ref[cs, :], preferred_element_type=F32)
    y_ref[...] = _rms(x1 + acc_ref[...], gf_ref[...])


def _out_ffn_specs(x, conv_prev, tm):
    conv_spec = pl.BlockSpec((None,) + conv_prev.shape[1:], lambda i, j, *_: (i, 0, 0))
    tok_spec = lambda width: pl.BlockSpec((None, tm, width), lambda i, j, *_: (i, j, 0))
    in_specs = [tok_spec(D_MODEL), tok_spec(D_ATT), tok_spec(D_SSM), conv_spec,
                _resident((D_MODEL, D_MODEL)), _resident((1, D_MODEL)),
                _resident((D_MODEL, 2 * D_FF)), _resident((3, D_FF)), _resident((1, D_FF)),
                _resident((D_FF, D_MODEL)), _resident((1, D_MODEL))]
    out_specs = [tok_spec(D_MODEL), conv_spec]
    out_shape = [jax.ShapeDtypeStruct(x.shape, F32), jax.ShapeDtypeStruct(conv_prev.shape, F32)]
    return in_specs, out_specs, out_shape, [pltpu.VMEM((tm, D_MODEL), F32)]


def _out_ffn(x, o, s, conv_prev, *ffn_w, tm, rows_are_positions):
    in_specs, out_specs, out_shape, scratch = _out_ffn_specs(x, conv_prev, tm)
    return pl.pallas_call(
        functools.partial(_out_ffn_body, rows_are_positions=rows_are_positions),
        grid=(x.shape[0], x.shape[1] // tm),
        in_specs=in_specs, out_specs=out_specs, out_shape=out_shape, scratch_shapes=scratch,
        compiler_params=_params(2), name="out_ffn")(x, o, s, conv_prev, *ffn_w)


N_FFN_REFS = 11


def _out_ffn_decode_body(pt_ref, *refs, per_tile, n_chunks, pages_per_chunk, n_pages):
    ffn_in, refs = refs[:N_FFN_REFS], refs[N_FFN_REFS:]
    (lamp_ref, sg_ref, q_ref, kn_ref, vn_ref, ck_hbm, cv_hbm), refs = refs[:7], refs[7:]
    (y_ref, cout_ref, od_ref, acc_ref), dec_scratch = refs[:4], refs[4:]
    tile = pl.program_id(0) * pl.num_programs(1) + pl.program_id(1)
    decode_chunk = _make_decode_stream(
        pt_ref, lamp_ref, sg_ref, q_ref, kn_ref, vn_ref, ck_hbm, cv_hbm, od_ref, *dec_scratch,
        tile=tile, n_tiles=pl.num_programs(0) * pl.num_programs(1), per_tile=per_tile, n_chunks=n_chunks,
        pages_per_chunk=pages_per_chunk, n_pages=n_pages)

    def before_chunk(c):
        for d in range(per_tile):
            if (d * N_FF_CHUNKS) // per_tile == c:
                decode_chunk(d)

    _out_ffn_body(*ffn_in, y_ref, cout_ref, acc_ref, rows_are_positions=True, before_chunk=before_chunk)


def _out_ffn_with_decode(x, o, s, conv_prev, ffn_w, page_table, lamp, subln_g, q, k_new, v_new,
                         cache_k, cache_v, *, tm, pages_per_chunk):
    n_tiles = x.shape[0] * (x.shape[1] // tm)
    n_dec, n_pages = page_table.shape
    n_chunks = n_pages // pages_per_chunk
    per_tile, rem = divmod(n_dec * n_chunks, n_tiles)
    assert rem == 0
    in_specs, out_specs, out_shape, scratch = _out_ffn_specs(x, conv_prev, tm)
    seq_rows = _resident((n_dec, D_QK))
    hbm = pl.BlockSpec(memory_space=pl.ANY)
    grid_spec = pltpu.PrefetchScalarGridSpec(
        num_scalar_prefetch=1,
        grid=(x.shape[0], x.shape[1] // tm),
        in_specs=in_specs + [_resident((4, HEAD_DIM)), _resident((1, V_DIM)), seq_rows, seq_rows, seq_rows, hbm, hbm],
        out_specs=out_specs + [pl.BlockSpec((n_dec, D_ATT), lambda *_: (0, 0))],
        scratch_shapes=scratch + _decode_scratch(pages_per_chunk))
    body = functools.partial(_out_ffn_decode_body, per_tile=per_tile, n_chunks=n_chunks,
                             pages_per_chunk=pages_per_chunk, n_pages=n_pages)
    return pl.pallas_call(
        body, grid_spec=grid_spec, out_shape=out_shape + [jax.ShapeDtypeStruct((n_dec, D_ATT), F32)],
        compiler_params=_params(2), name="out_ffn_decode")(
            page_table.reshape(-1), x, o, s, conv_prev, *ffn_w, lamp, subln_g, q, k_new, v_new, cache_k, cache_v)


def kernel(x_prompt, x_sample, cache_k, cache_v, state_ssm_re, state_ssm_im, cache_ffn_conv, page_table,
           norm1_g, w_in, lam_q1, lam_k1, lam_q2, lam_k2, subln_g, ssm_lam_re, ssm_lam_im, ssm_log_dt,
           ssm_b_re, ssm_b_im, ssm_c_re, ssm_c_im, ssm_d, w_glu, b_glu, w_out, norm2_g, w_up, conv_w,
           conv_b, w_down, final_g):
    assert w_in.shape[0] == 1, "single-layer step"
    n_batch, seq, _ = x_prompt.shape
    n_dec = x_sample.shape[0]
    n_pool = cache_k.shape[1]

    w_in_bf, w_out_bf = w_in[0].astype(BF16), w_out[0].astype(BF16)
    w_up_bf, w_down_bf, w_glu_bf = w_up[0].astype(BF16), w_down[0].astype(BF16), w_glu[0].astype(BF16)
    g1, g2, gf = norm1_g[0].reshape(1, -1), norm2_g[0].reshape(1, -1), final_g.reshape(1, -1)
    lamp = jnp.stack([lam_q1[0], lam_k1[0], lam_q2[0], lam_k2[0]])
    sg = subln_g[0].reshape(1, V_DIM)
    a_re, a_im, bb_re, bb_im = _ssm_discretize(ssm_lam_re[0], ssm_lam_im[0], ssm_log_dt[0],
                                               ssm_b_re[0], ssm_b_im[0])
    ssm_w = (a_re, a_im, _blockdiag_in(bb_re).astype(BF16), _blockdiag_in(bb_im).astype(BF16),
             _blockdiag_out(ssm_c_re[0]).astype(BF16), _blockdiag_out(ssm_c_im[0]).astype(BF16),
             ssm_d[0].reshape(1, -1), w_glu_bf, b_glu[0].reshape(1, -1))
    ffn_w = (w_out_bf, g2, w_up_bf, conv_w[0], conv_b[0].reshape(1, -1), w_down_bf, gf)

    q, k, v, u, k_bf, v_bf = _in_proj(x_prompt, g1, w_in_bf, tm=512, q_dtype=BF16, bf16_kv=True)
    o = _prompt_attention(lamp, sg, q, k_bf, v_bf, tq=256, tk=512)
    s, h_p = _ssm_glu(u, jnp.zeros((n_batch, 2 * N_STATE), F32), *ssm_w, steps_per_block=64)

    xs = x_sample.reshape(1, n_dec, D_MODEL)
    qs, ks, vs, us = _in_proj(xs, g1, w_in_bf, tm=n_dec, q_dtype=F32, bf16_kv=False)
    seq_rows = lambda t: t.reshape(n_dec, D_QK)
    y_p, conv_p, o_s = _out_ffn_with_decode(
        x_prompt, o, s, jnp.zeros((n_batch, 2, D_FF), F32), ffn_w,
        page_table, lamp, sg, seq_rows(qs), seq_rows(ks), seq_rows(vs),
        cache_k.transpose(0, 1, 3, 4, 5, 2).reshape(n_pool, D_QK, PAGE),
        cache_v.reshape(n_pool, PAGE * N_HEADS, V_DIM),
        tm=256, pages_per_chunk=16)
    h0_s = jnp.concatenate([state_ssm_re[0].reshape(n_dec, N_STATE),
                            state_ssm_im[0].reshape(n_dec, N_STATE)], axis=-1)
    s_s, h_s = _ssm_glu(us.reshape(n_dec, D_SSM), h0_s, *ssm_w, steps_per_block=1)
    y_s, conv_s = _out_ffn(xs, o_s.reshape(1, n_dec, D_ATT).astype(BF16), s_s.reshape(1, n_dec, D_SSM),
                           cache_ffn_conv[0].reshape(1, n_dec, 2 * D_FF), *ffn_w,
                           tm=n_dec, rows_are_positions=False)

    heads_k = lambda t, lead: t.reshape(1, *lead, N_HEADS, 2, HEAD_DIM)
    heads_v = lambda t, lead: t.reshape(1, *lead, N_HEADS, V_DIM)
    state = lambda h, n: h.reshape(1, n, N_GROUPS, SSM_STATE)
    return (y_p, y_s.reshape(n_dec, 1, D_MODEL),
            heads_k(k, (n_batch, seq)), heads_v(v, (n_batch, seq)),
            heads_k(ks, (n_dec, 1)), heads_v(vs, (n_dec, 1)),
            state(h_p[:, :N_STATE], n_batch), state(h_p[:, N_STATE:], n_batch),
            state(h_s[:, :N_STATE], n_dec), state(h_s[:, N_STATE:], n_dec),
            conv_p.reshape(1, n_batch, 2, D_FF), conv_s.reshape(1, n_dec, 2, D_FF))
```

```python
import functools
import math

import jax
import jax.numpy as jnp
from jax import lax
from jax.experimental import pallas as pl
from jax.experimental.pallas import tpu as pltpu

F32 = jnp.float32
BF16 = jnp.bfloat16

D_MODEL = 1024
N_HEADS = 4
HEAD_DIM = 64
V_DIM = 2 * HEAD_DIM
D_ATT = N_HEADS * V_DIM
D_QK = N_HEADS * 2 * HEAD_DIM
D_SSM = 512
SSM_GROUP = 16
N_GROUPS = D_SSM // SSM_GROUP
SSM_STATE = 64
N_STATE = N_GROUPS * SSM_STATE
D_IN = 2 * D_QK + D_ATT + D_SSM
D_FF = 2816
PAGE = 128
EPS = 1e-6
LAM_INIT = 0.8 - 0.6 * math.exp(-0.3 * 0)
QK_SCALE = HEAD_DIM ** -0.5

LANES = 128
SUBLANES = 8
MXU_COLS = 256
VMEM_LIMIT = 56 * 1024 * 1024

FF_CHUNK = MXU_COLS
N_FF_CHUNKS = D_FF // FF_CHUNK
SSM_CHUNK_GROUPS = MXU_COLS // SSM_GROUP
SSM_CHUNK_STATES = SSM_CHUNK_GROUPS * SSM_STATE
N_SSM_CHUNKS = D_SSM // MXU_COLS
SCAN_COLS = 512


def _params(n_axes, vmem=VMEM_LIMIT):
    return pltpu.CompilerParams(dimension_semantics=("arbitrary",) * n_axes, vmem_limit_bytes=vmem)


def _rms(x, g):
    return x * lax.rsqrt(jnp.mean(x * x, axis=-1, keepdims=True) + EPS) * g


def _sigmoid(x):
    return 1.0 / (1.0 + jnp.exp(-x))


def _diff_lambda(lamp_ref):
    lp = lamp_ref[...]
    a1 = jnp.sum(lp[0:1] * lp[1:2], axis=-1, keepdims=True)
    a2 = jnp.sum(lp[2:3] * lp[3:4], axis=-1, keepdims=True)
    return jnp.exp(a1) - jnp.exp(a2) + LAM_INIT


def _resident(shape):
    return pl.BlockSpec(shape, lambda *_: (0,) * len(shape), pipeline_mode=pl.Buffered(1))


def _disc_body(lr_ref, li_ref, ldt_ref, br_ref, bi_ref, ar_ref, ai_ref, bbr_ref, bbi_ref):
    lr, li = lr_ref[...], li_ref[...]
    dt = jnp.exp(ldt_ref[...])
    mag = jnp.exp(lr * dt)
    a_re, a_im = mag * jnp.cos(li * dt), mag * jnp.sin(li * dt)
    den = lr * lr + li * li
    nr, ni = a_re - 1.0, a_im
    coef_re = (nr * lr + ni * li) / den
    coef_im = (ni * lr - nr * li) / den
    br, bi = br_ref[...], bi_ref[...]
    bbr_ref[...] = coef_re * br - coef_im * bi
    bbi_ref[...] = coef_re * bi + coef_im * br
    ar_ref[...] = a_re
    ai_ref[...] = a_im


def _ssm_discretize(lam_re, lam_im, log_dt, b_re, b_im):
    row = jax.ShapeDtypeStruct((1, N_STATE), F32)
    mat = jax.ShapeDtypeStruct((SSM_GROUP, N_STATE), F32)
    return pl.pallas_call(_disc_body, out_shape=(row, row, mat, mat), name="ssm_discretize")(
        lam_re.reshape(1, N_STATE), lam_im.reshape(1, N_STATE),
        jnp.repeat(log_dt, SSM_STATE).reshape(1, N_STATE),
        b_re.transpose(2, 0, 1).reshape(SSM_GROUP, N_STATE),
        b_im.transpose(2, 0, 1).reshape(SSM_GROUP, N_STATE))


def _blockdiag_in(bbt):
    t = bbt.reshape(SSM_GROUP, N_SSM_CHUNKS, SSM_CHUNK_GROUPS, SSM_STATE)
    eye = jnp.eye(SSM_CHUNK_GROUPS, dtype=bbt.dtype)
    return jnp.einsum('cngp,gh->ngchp', t, eye).reshape(N_SSM_CHUNKS, MXU_COLS, SSM_CHUNK_STATES)


def _blockdiag_out(c):
    t = c.reshape(N_SSM_CHUNKS, SSM_CHUNK_GROUPS, SSM_GROUP, SSM_STATE)
    eye = jnp.eye(SSM_CHUNK_GROUPS, dtype=c.dtype)
    return jnp.einsum('ngkp,gh->ngphk', t, eye).reshape(N_SSM_CHUNKS, SSM_CHUNK_STATES, MXU_COLS)


def _inproj_body(x_ref, g_ref, w_ref, q_ref, k_ref, v_ref, u_ref, *maybe_bf16_kv):
    xn = _rms(x_ref[...], g_ref[...]).astype(BF16)

    def proj(lo):
        return jnp.dot(xn, w_ref[:, lo:lo + D_QK], preferred_element_type=F32)

    q_ref[...] = (proj(0) * QK_SCALE).astype(q_ref.dtype)
    k = proj(D_QK)
    v = proj(2 * D_QK)
    k_ref[...] = k
    for h in range(N_HEADS):
        v_ref[pl.ds(h, x_ref.shape[0], stride=N_HEADS), :] = v[:, h * V_DIM:(h + 1) * V_DIM]
    u_ref[...] = proj(2 * D_QK + D_ATT)
    if maybe_bf16_kv:
        kb_ref, vb_ref = maybe_bf16_kv
        kb_ref[...] = k.astype(BF16)
        vb_ref[...] = v.astype(BF16)


def _in_proj(x, g, w_bf, *, tm, q_dtype, bf16_kv):
    b, s, _ = x.shape
    tok = lambda dt: jax.ShapeDtypeStruct((b, s, D_QK), dt)
    tok_spec = pl.BlockSpec((None, tm, D_QK), lambda i, j: (i, j, 0))
    out_shape = [tok(q_dtype), tok(F32), jax.ShapeDtypeStruct((b, s * N_HEADS, V_DIM), F32), tok(F32)]
    out_specs = [tok_spec, tok_spec, pl.BlockSpec((None, tm * N_HEADS, V_DIM), lambda i, j: (i, j, 0)), tok_spec]
    if bf16_kv:
        out_shape += [tok(BF16), tok(BF16)]
        out_specs += [tok_spec, tok_spec]
    return pl.pallas_call(
        _inproj_body,
        grid=(b, s // tm),
        in_specs=[pl.BlockSpec((None, tm, D_MODEL), lambda i, j: (i, j, 0)),
                  _resident((1, D_MODEL)), _resident((D_MODEL, D_IN))],
        out_specs=out_specs, out_shape=out_shape,
        compiler_params=_params(2), name="in_proj")(x, g, w_bf)


POS_RADIX = 256


def _alibi_key_features(seq):
    j = lax.broadcasted_iota(jnp.int32, (seq, LANES), 0)
    lane = lax.broadcasted_iota(jnp.int32, (seq, LANES), 1)
    feat = jnp.where(lane == 0, j // POS_RADIX, jnp.where(lane == 1, j % POS_RADIX, 0))
    return feat.astype(BF16)


def _prompt_attn_body(lamp_ref, sg_ref, q_ref, k_ref, v_ref, pos_ref, o_ref, m_scr, acc_scr, qs_scr, s_scr,
                      *, tq, tk):
    assert tk == 2 * tq
    t = pl.program_id(1)
    lam = _diff_lambda(lamp_ref)
    lane = lax.broadcasted_iota(jnp.int32, (1, LANES), 1)
    ones_col = jnp.broadcast_to(jnp.where(lane == 0, 1.0, 0.0).astype(BF16), (tk, LANES))
    nt = (((1,), (1,)), ((), ()))
    heads = [slice(h * V_DIM, (h + 1) * V_DIM) for h in range(N_HEADS)]

    def stack_queries(half):
        for h, hs in enumerate(heads):
            slope = 2.0 ** (-8.0 * (h + 1) / N_HEADS)
            q = q_ref[half * tq:(half + 1) * tq, hs]
            zero = jnp.zeros_like(q)
            q_pos = jnp.broadcast_to(
                jnp.where(lane == 0, POS_RADIX * slope, jnp.where(lane == 1, slope, 0.0)).astype(BF16),
                (tq, LANES))
            qs_scr[half, h, :tq, :LANES] = jnp.where(lane < HEAD_DIM, q, zero)
            qs_scr[half, h, tq:, :LANES] = jnp.where(lane >= HEAD_DIM, q, zero)
            qs_scr[half, h, :tq, LANES:] = q_pos
            qs_scr[half, h, tq:, LANES:] = q_pos

    def scores(half, j, slot):
        rows = pl.ds(pl.multiple_of(j * tk, tk), tk)
        pos = pos_ref[rows, :]
        for h, hs in enumerate(heads):
            kx = jnp.concatenate([k_ref[rows, hs], pos], axis=1)
            s_scr[slot, h] = lax.dot_general(qs_scr[half, h], kx, nt, preferred_element_type=F32)

    def absorb(j, slot, width, causal):
        rows = pl.ds(pl.multiple_of(j * tk, tk), width)
        for h, hs in enumerate(heads):
            vx = jnp.concatenate([v_ref[rows, hs], ones_col[:width]], axis=1)
            s = s_scr[slot, h, :, :width]
            if causal is not None:
                s = jnp.where(causal, s, -jnp.inf)
            m_old = m_scr[h]
            m_new = jnp.maximum(m_old, jnp.max(s, axis=-1, keepdims=True))
            alpha = jnp.exp(m_old - m_new)
            p = jnp.exp(s - jnp.concatenate([m_new] * (width // LANES), axis=1)).astype(BF16)
            acc_scr[h] = (jnp.concatenate([alpha, alpha], axis=1) * acc_scr[h]
                          + jnp.dot(p, vx, preferred_element_type=F32))
            m_scr[h] = m_new

    def causal(width):
        r = lax.broadcasted_iota(jnp.int32, (2 * tq, width), 0) % tq
        c = lax.broadcasted_iota(jnp.int32, (2 * tq, width), 1)
        return r + (width - tq) >= c

    def reset():
        m_scr[...] = jnp.full(m_scr.shape, -jnp.inf, F32)
        acc_scr[...] = jnp.zeros(acc_scr.shape, F32)

    def finalize(half):
        for h, hs in enumerate(heads):
            acc = acc_scr[h]
            a1, a2 = acc[:tq], acc[tq:]
            o = (a1[:, :V_DIM] / a1[:, V_DIM:V_DIM + 1]
                 - lam * (a2[:, :V_DIM] / a2[:, V_DIM:V_DIM + 1]))
            o_ref[half * tq:(half + 1) * tq, hs] = (_rms(o, sg_ref[...]) * (1.0 - LAM_INIT)).astype(o_ref.dtype)

    def run(half, slot0):
        def pipelined(j, carry):
            absorb(j, (slot0 + j) % 2, tk, None)
            scores(half, j + 1, (slot0 + j + 1) % 2)
            return carry
        lax.fori_loop(0, t, pipelined, 0)

    stack_queries(0)
    scores(0, 0, 0)
    reset()
    run(0, 0)
    absorb(t, t % 2, tq, causal(tq))
    stack_queries(1)
    scores(1, 0, (t + 1) % 2)
    finalize(0)
    reset()
    run(1, t + 1)
    absorb(t, 1, tk, causal(tk))
    finalize(1)


def _prompt_attention(lamp, subln_g, q, k, v, *, tq, tk):
    b, s, _ = q.shape
    assert s % tk == 0 and tk % tq == 0
    seq_spec = pl.BlockSpec((None, s, D_QK), lambda i, j: (i, 0, 0))
    return pl.pallas_call(
        functools.partial(_prompt_attn_body, tq=tq, tk=tk),
        grid=(b, s // (2 * tq)),
        in_specs=[_resident((4, HEAD_DIM)), _resident((1, V_DIM)),
                  pl.BlockSpec((None, 2 * tq, D_QK), lambda i, j: (i, j, 0)), seq_spec, seq_spec,
                  _resident((s, LANES))],
        out_specs=pl.BlockSpec((None, 2 * tq, D_ATT), lambda i, j: (i, j, 0)),
        out_shape=jax.ShapeDtypeStruct((b, s, D_ATT), BF16),
        scratch_shapes=[pltpu.VMEM((N_HEADS, 2 * tq, LANES), F32), pltpu.VMEM((N_HEADS, 2 * tq, 2 * LANES), F32),
                        pltpu.VMEM((2, N_HEADS, 2 * tq, 2 * LANES), BF16),
                        pltpu.VMEM((2, N_HEADS, 2 * tq, tk), F32)],
        compiler_params=_params(2), name="prompt_attn")(lamp, subln_g, q, k, v, _alibi_key_features(s))


N_MAPS = 2 * N_HEADS
DECODE_SLOTS = 3


def _decode_scratch(pages_per_chunk):
    chunk_tokens = pages_per_chunk * PAGE
    return [pltpu.VMEM((DECODE_SLOTS, D_QK, chunk_tokens), F32),
            pltpu.VMEM((DECODE_SLOTS, chunk_tokens * N_HEADS, V_DIM), F32),
            pltpu.SemaphoreType.DMA((2, DECODE_SLOTS)),
            pltpu.VMEM((N_MAPS, 1), F32), pltpu.VMEM((N_MAPS, 1), F32), pltpu.VMEM((N_MAPS, D_ATT), F32)]


def _make_decode_stream(pt_ref, lamp_ref, sg_ref, q_ref, kn_ref, vn_ref, ck_hbm, cv_hbm, o_ref,
                        kbuf, vbuf, sem, m_ref, l_ref, acc_ref, *, tile, n_tiles, per_tile, n_chunks,
                        pages_per_chunk, n_pages):
    assert per_tile % n_chunks == 0 and DECODE_SLOTS - 1 <= per_tile
    seqs_per_tile = per_tile // n_chunks
    chunk_tokens = pages_per_chunk * PAGE
    past_len = n_pages * PAGE
    row = lax.broadcasted_iota(jnp.int32, (N_MAPS, D_QK), 0)
    lane = lax.broadcasted_iota(jnp.int32, (N_MAPS, D_QK), 1)
    row1 = lax.broadcasted_iota(jnp.int32, (N_MAPS, 1), 0)
    slope = jnp.exp2(-8.0 * ((row1 // 2) + 1).astype(F32) / N_HEADS)
    heads = [slice(h * V_DIM, (h + 1) * V_DIM) for h in range(N_HEADS)]

    def coords(c):
        t, c = tile + c // per_tile, c % per_tile
        return t * seqs_per_tile + c // n_chunks, c % n_chunks, (t * per_tile + c) % DECODE_SLOTS

    def page_copies(seq, chunk, slot):
        copies = []
        for p in range(pages_per_chunk):
            page = pt_ref[seq * n_pages + chunk * pages_per_chunk + p]
            copies.append(pltpu.make_async_copy(
                ck_hbm.at[page], kbuf.at[slot, :, pl.ds(p * PAGE, PAGE)], sem.at[0, slot]))
            copies.append(pltpu.make_async_copy(
                cv_hbm.at[page], vbuf.at[slot, pl.ds(p * PAGE * N_HEADS, PAGE * N_HEADS), :], sem.at[1, slot]))
        return copies

    def start(copies):
        for cp in copies:
            cp.start()

    def decode_chunk(c):
        seq, chunk, slot = coords(c)
        ahead = DECODE_SLOTS - 1
        if c == 0:
            @pl.when(tile == 0)
            def _():
                for first_chunks in range(ahead):
                    start(page_copies(*coords(first_chunks)))
        if c + ahead < per_tile:
            start(page_copies(*coords(c + ahead)))
        else:
            @pl.when(tile + 1 < n_tiles)
            def _():
                start(page_copies(*coords(c + ahead)))
        for cp in page_copies(seq, chunk, slot):
            cp.wait()

        q = q_ref[pl.ds(seq, 1), :]
        qblk = jnp.where(lane // HEAD_DIM == row, jnp.broadcast_to(q, (N_MAPS, D_QK)), 0.0)
        kt = kbuf[slot].astype(BF16)
        s = jnp.dot(qblk.astype(BF16), kt, preferred_element_type=F32)
        pos = chunk * chunk_tokens + lax.broadcasted_iota(jnp.int32, (1, chunk_tokens), 1)
        s = s - slope * (past_len - pos).astype(F32)
        s_max = jnp.max(s, axis=-1, keepdims=True)
        first, last = chunk == 0, chunk == n_chunks - 1
        m_new = s_max if first else jnp.maximum(m_ref[...], s_max)
        p = jnp.exp(s - m_new)
        l_new = jnp.sum(p, axis=-1, keepdims=True)
        if not first:
            alpha = jnp.exp(m_ref[...] - m_new)
            l_new = alpha * l_ref[...] + l_new
        pb = p.astype(BF16)
        acc = []
        for h, hs in enumerate(heads):
            vh = vbuf[slot, pl.ds(h, chunk_tokens, stride=N_HEADS), :].astype(BF16)
            pv = jnp.dot(pb, vh, preferred_element_type=F32)
            acc.append(pv if first else alpha * acc_ref[:, hs] + pv)
        if not last:
            m_ref[...] = m_new
            l_ref[...] = l_new
            for hs, a in zip(heads, acc):
                acc_ref[:, hs] = a
            return
        s_new = jnp.sum(qblk * kn_ref[pl.ds(seq, 1), :], axis=-1, keepdims=True)
        m_fin = jnp.maximum(m_new, s_new)
        alpha = jnp.exp(m_new - m_fin)
        p_new = jnp.exp(s_new - m_fin)
        l_fin = alpha * l_new + p_new
        acc = alpha * jnp.concatenate(acc, axis=-1) + p_new * vn_ref[pl.ds(seq, 1), :]
        lam = _diff_lambda(lamp_ref)
        coef = jnp.where(row1 % 2 == 0, 1.0, -lam) / l_fin
        own = lane // V_DIM == row // 2
        o = jnp.sum(jnp.where(own, coef * acc, 0.0), axis=0, keepdims=True)
        o_ref[pl.ds(seq, 1), :] = jnp.concatenate(
            [_rms(o[:, hs], sg_ref[...]) * (1.0 - LAM_INIT) for hs in heads], axis=-1)

    return decode_chunk


def _ssm_body(u_ref, h0_ref, ar_ref, ai_ref, bre_ref, bim_ref, cre_ref, cim_ref, d_ref, wg_ref, bg_ref,
              s_ref, h_ref, x_scr, *maybe_regroup_scr, rows_per_step, n_steps):
    r = rows_per_step
    slabs = [slice(j * LANES, (j + 1) * LANES) for j in range(D_SSM // LANES)]

    @pl.when(pl.program_id(0) == 0)
    def _():
        h_ref[...] = h0_ref[...]

    if maybe_regroup_scr:
        regroup_scr, = maybe_regroup_scr
        for b in range(r):
            for j, sl in enumerate(slabs):
                regroup_scr[j, pl.ds(b, n_steps, stride=r), :] = u_ref[b, :, sl]
        u = jnp.concatenate([regroup_scr[j] for j in range(len(slabs))], axis=-1)
    else:
        u = u_ref[...]
    ub = u.astype(BF16)
    for n in range(N_SSM_CHUNKS):
        un = ub[:, n * MXU_COLS:(n + 1) * MXU_COLS]
        cols = slice(n * SSM_CHUNK_STATES, (n + 1) * SSM_CHUNK_STATES)
        x_scr[:, cols] = jnp.dot(un, bre_ref[n], preferred_element_type=F32)
        x_scr[:, N_STATE + n * SSM_CHUNK_STATES:N_STATE + (n + 1) * SSM_CHUNK_STATES] = jnp.dot(
            un, bim_ref[n], preferred_element_type=F32)

    for cb in range(N_STATE // SCAN_COLS):
        re = slice(cb * SCAN_COLS, (cb + 1) * SCAN_COLS)
        im = slice(N_STATE + cb * SCAN_COLS, N_STATE + (cb + 1) * SCAN_COLS)
        a_r = jnp.broadcast_to(ar_ref[:, re], (r, SCAN_COLS))
        a_i = jnp.broadcast_to(ai_ref[:, re], (r, SCAN_COLS))

        def step(t, carry, re=re, im=im, a_r=a_r, a_i=a_i):
            hr, hi = carry
            rows = pl.ds(pl.multiple_of(t * r, r), r)
            nhr = a_r * hr - a_i * hi + x_scr[rows, re]
            nhi = a_r * hi + a_i * hr + x_scr[rows, im]
            x_scr[rows, re] = nhr
            x_scr[rows, im] = nhi
            return nhr, nhi

        carry = (h_ref[:, re], h_ref[:, im])
        if n_steps == 1:
            hr, hi = step(0, carry)
        else:
            hr, hi = lax.fori_loop(0, n_steps, step, carry, unroll=True)
        h_ref[:, re] = hr
        h_ref[:, im] = hi

    ys = []
    for n in range(N_SSM_CHUNKS):
        cols = slice(n * SSM_CHUNK_STATES, (n + 1) * SSM_CHUNK_STATES)
        cols_im = slice(N_STATE + n * SSM_CHUNK_STATES, N_STATE + (n + 1) * SSM_CHUNK_STATES)
        ys.append(jnp.dot(x_scr[:, cols].astype(BF16), cre_ref[n], preferred_element_type=F32)
                  - jnp.dot(x_scr[:, cols_im].astype(BF16), cim_ref[n], preferred_element_type=F32))
    y = jnp.concatenate(ys, axis=-1) + d_ref[...] * u
    g = 0.5 * y * (1.0 + jnp.tanh(math.sqrt(2.0 / math.pi) * (y + 0.044715 * (y * y * y))))
    z = jnp.dot(g.astype(BF16), wg_ref[...], preferred_element_type=F32) + bg_ref[...]
    s = g * _sigmoid(z)
    if maybe_regroup_scr:
        for j, sl in enumerate(slabs):
            regroup_scr[j] = s[:, sl]
        for b in range(r):
            s_ref[b] = jnp.concatenate([regroup_scr[j, pl.ds(b, n_steps, stride=r), :]
                                        for j in range(len(slabs))], axis=-1).astype(s_ref.dtype)
    else:
        s_ref[...] = s.astype(s_ref.dtype)


def _ssm_glu(u, h0, a_re, a_im, b_re_bd, b_im_bd, c_re_bd, c_im_bd, d_skip, w_glu_bf, b_glu,
             *, steps_per_block):
    rows_per_step = u.shape[0]
    blk = rows_per_step * steps_per_block
    body = functools.partial(_ssm_body, rows_per_step=rows_per_step, n_steps=steps_per_block)
    if u.ndim == 3:
        n_blocks = u.shape[1] // steps_per_block
        io_spec = pl.BlockSpec((rows_per_step, steps_per_block, D_SSM), lambda i: (0, i, 0))
        scratch = [pltpu.VMEM((D_SSM // LANES, blk, LANES), F32)]
    else:
        assert steps_per_block == 1
        n_blocks = 1
        io_spec = pl.BlockSpec((rows_per_step, D_SSM), lambda i: (0, 0))
        scratch = []
    return pl.pallas_call(
        body,
        grid=(n_blocks,),
        in_specs=[io_spec,
                  _resident((rows_per_step, 2 * N_STATE)),
                  _resident((1, N_STATE)), _resident((1, N_STATE)),
                  _resident((N_SSM_CHUNKS, MXU_COLS, SSM_CHUNK_STATES)),
                  _resident((N_SSM_CHUNKS, MXU_COLS, SSM_CHUNK_STATES)),
                  _resident((N_SSM_CHUNKS, SSM_CHUNK_STATES, MXU_COLS)),
                  _resident((N_SSM_CHUNKS, SSM_CHUNK_STATES, MXU_COLS)),
                  _resident((1, D_SSM)), _resident((D_SSM, D_SSM)), _resident((1, D_SSM))],
        out_specs=[io_spec, pl.BlockSpec((rows_per_step, 2 * N_STATE), lambda i: (0, 0))],
        out_shape=[jax.ShapeDtypeStruct(u.shape, BF16),
                   jax.ShapeDtypeStruct((rows_per_step, 2 * N_STATE), F32)],
        scratch_shapes=[pltpu.VMEM((blk, 2 * N_STATE), F32)] + scratch,
        compiler_params=_params(1), name="ssm_glu")(
            u, h0, a_re, a_im, b_re_bd, b_im_bd, c_re_bd, c_im_bd, d_skip, w_glu_bf, b_glu)


def _out_ffn_body(x_ref, o_ref, s_ref, cprev_ref, wo_ref, g2_ref, wup_ref, cw_ref, cb_ref, wdn_ref, gf_ref,
                  y_ref, cout_ref, acc_ref, *, rows_are_positions, before_chunk=None):
    tm = x_ref.shape[0]
    x1 = (x_ref[...]
          + jnp.dot(o_ref[...], wo_ref[:D_ATT, :], preferred_element_type=F32)
          + jnp.dot(s_ref[...], wo_ref[D_ATT:, :], preferred_element_type=F32))
    xn = _rms(x1, g2_ref[...]).astype(BF16)

    if rows_are_positions:
        @pl.when(pl.program_id(1) == 0)
        def _():
            cout_ref[...] = cprev_ref[...]
        hist = cout_ref[...]
        rid = lax.broadcasted_iota(jnp.int32, (tm, FF_CHUNK), 0)
    else:
        hist = cprev_ref[...]

    acc_ref[...] = jnp.zeros(acc_ref.shape, F32)
    for c in range(N_FF_CHUNKS):
        if before_chunk is not None:
            before_chunk(c)
        cs = slice(c * FF_CHUNK, (c + 1) * FF_CHUNK)
        cs_up = slice(D_FF + c * FF_CHUNK, D_FF + (c + 1) * FF_CHUNK)
        gate = jnp.dot(xn, wup_ref[:, cs], preferred_element_type=F32)
        up = jnp.dot(xn, wup_ref[:, cs_up], preferred_element_type=F32)
        if rows_are_positions:
            p0, p1 = hist[0:1, cs], hist[1:2, cs]
            g1 = jnp.where(rid == 0, p1, pltpu.roll(gate, 1, 0))
            g2 = jnp.where(rid == 0, p0, jnp.where(rid == 1, p1, pltpu.roll(gate, 2, 0)))
            cout_ref[:, cs] = gate[tm - 2:, :]
        else:
            g2, g1 = hist[:, cs], hist[:, cs_up]
            cout_ref[:, cs] = g1
            cout_ref[:, cs_up] = gate
        gc = cb_ref[:, cs] + cw_ref[0:1, cs] * g2 + cw_ref[1:2, cs] * g1 + cw_ref[2:3, cs] * gate
        act = (gc * _sigmoid(gc) * up).astype(BF16)
        acc_ref[...] += jnp.dot(act, wdn_ref[cs, :], preferred_element_type=F32)
    y_ref[...] = _rms(x1 + acc_ref[...], gf_ref[...])


def _out_ffn_specs(x, conv_prev, tm):
    conv_spec = pl.BlockSpec((None,) + conv_prev.shape[1:], lambda i, j, *_: (i, 0, 0))
    tok_spec = lambda width: pl.BlockSpec((None, tm, width), lambda i, j, *_: (i, j, 0))
    in_specs = [tok_spec(D_MODEL), tok_spec(D_ATT), tok_spec(D_SSM), conv_spec,
                _resident((D_MODEL, D_MODEL)), _resident((1, D_MODEL)),
                _resident((D_MODEL, 2 * D_FF)), _resident((3, D_FF)), _resident((1, D_FF)),
                _resident((D_FF, D_MODEL)), _resident((1, D_MODEL))]
    out_specs = [tok_spec(D_MODEL), conv_spec]
    out_shape = [jax.ShapeDtypeStruct(x.shape, F32), jax.ShapeDtypeStruct(conv_prev.shape, F32)]
    return in_specs, out_specs, out_shape, [pltpu.VMEM((tm, D_MODEL), F32)]


def _out_ffn(x, o, s, conv_prev, *ffn_w, tm, rows_are_positions):
    in_specs, out_specs, out_shape, scratch = _out_ffn_specs(x, conv_prev, tm)
    return pl.pallas_call(
        functools.partial(_out_ffn_body, rows_are_positions=rows_are_positions),
        grid=(x.shape[0], x.shape[1] // tm),
        in_specs=in_specs, out_specs=out_specs, out_shape=out_shape, scratch_shapes=scratch,
        compiler_params=_params(2), name="out_ffn")(x, o, s, conv_prev, *ffn_w)


N_FFN_REFS = 11


def _out_ffn_decode_body(pt_ref, *refs, per_tile, n_chunks, pages_per_chunk, n_pages):
    ffn_in, refs = refs[:N_FFN_REFS], refs[N_FFN_REFS:]
    (lamp_ref, sg_ref, q_ref, kn_ref, vn_ref, ck_hbm, cv_hbm), refs = refs[:7], refs[7:]
    (y_ref, cout_ref, od_ref, acc_ref), dec_scratch = refs[:4], refs[4:]
    tile = pl.program_id(0) * pl.num_programs(1) + pl.program_id(1)
    decode_chunk = _make_decode_stream(
        pt_ref, lamp_ref, sg_ref, q_ref, kn_ref, vn_ref, ck_hbm, cv_hbm, od_ref, *dec_scratch,
        tile=tile, n_tiles=pl.num_programs(0) * pl.num_programs(1), per_tile=per_tile, n_chunks=n_chunks,
        pages_per_chunk=pages_per_chunk, n_pages=n_pages)

    def before_chunk(c):
        for d in range(per_tile):
            if (d * N_FF_CHUNKS) // per_tile == c:
                decode_chunk(d)

    _out_ffn_body(*ffn_in, y_ref, cout_ref, acc_ref, rows_are_positions=True, before_chunk=before_chunk)


def _out_ffn_with_decode(x, o, s, conv_prev, ffn_w, page_table, lamp, subln_g, q, k_new, v_new,
                         cache_k, cache_v, *, tm, pages_per_chunk):
    n_tiles = x.shape[0] * (x.shape[1] // tm)
    n_dec, n_pages = page_table.shape
    n_chunks = n_pages // pages_per_chunk
    per_tile, rem = divmod(n_dec * n_chunks, n_tiles)
    assert rem == 0
    in_specs, out_specs, out_shape, scratch = _out_ffn_specs(x, conv_prev, tm)
    seq_rows = _resident((n_dec, D_QK))
    hbm = pl.BlockSpec(memory_space=pl.ANY)
    grid_spec = pltpu.PrefetchScalarGridSpec(
        num_scalar_prefetch=1,
        grid=(x.shape[0], x.shape[1] // tm),
        in_specs=in_specs + [_resident((4, HEAD_DIM)), _resident((1, V_DIM)), seq_rows, seq_rows, seq_rows, hbm, hbm],
        out_specs=out_specs + [pl.BlockSpec((n_dec, D_ATT), lambda *_: (0, 0))],
        scratch_shapes=scratch + _decode_scratch(pages_per_chunk))
    body = functools.partial(_out_ffn_decode_body, per_tile=per_tile, n_chunks=n_chunks,
                             pages_per_chunk=pages_per_chunk, n_pages=n_pages)
    return pl.pallas_call(
        body, grid_spec=grid_spec, out_shape=out_shape + [jax.ShapeDtypeStruct((n_dec, D_ATT), F32)],
        compiler_params=_params(2), name="out_ffn_decode")(
            page_table.reshape(-1), x, o, s, conv_prev, *ffn_w, lamp, subln_g, q, k_new, v_new, cache_k, cache_v)


def kernel(x_prompt, x_sample, cache_k, cache_v, state_ssm_re, state_ssm_im, cache_ffn_conv, page_table,
           norm1_g, w_in, lam_q1, lam_k1, lam_q2, lam_k2, subln_g, ssm_lam_re, ssm_lam_im, ssm_log_dt,
           ssm_b_re, ssm_b_im, ssm_c_re, ssm_c_im, ssm_d, w_glu, b_glu, w_out, norm2_g, w_up, conv_w,
           conv_b, w_down, final_g):
    assert w_in.shape[0] == 1, "single-layer step"
    n_batch, seq, _ = x_prompt.shape
    n_dec = x_sample.shape[0]
    n_pool = cache_k.shape[1]

    w_in_bf, w_out_bf = w_in[0].astype(BF16), w_out[0].astype(BF16)
    w_up_bf, w_down_bf, w_glu_bf = w_up[0].astype(BF16), w_down[0].astype(BF16), w_glu[0].astype(BF16)
    g1, g2, gf = norm1_g[0].reshape(1, -1), norm2_g[0].reshape(1, -1), final_g.reshape(1, -1)
    lamp = jnp.stack([lam_q1[0], lam_k1[0], lam_q2[0], lam_k2[0]])
    sg = subln_g[0].reshape(1, V_DIM)
    a_re, a_im, bb_re, bb_im = _ssm_discretize(ssm_lam_re[0], ssm_lam_im[0], ssm_log_dt[0],
                                               ssm_b_re[0], ssm_b_im[0])
    ssm_w = (a_re, a_im, _blockdiag_in(bb_re).astype(BF16), _blockdiag_in(bb_im).astype(BF16),
             _blockdiag_out(ssm_c_re[0]).astype(BF16), _blockdiag_out(ssm_c_im[0]).astype(BF16),
             ssm_d[0].reshape(1, -1), w_glu_bf, b_glu[0].reshape(1, -1))
    ffn_w = (w_out_bf, g2, w_up_bf, conv_w[0], conv_b[0].reshape(1, -1), w_down_bf, gf)

    q, k, v, u, k_bf, v_bf = _in_proj(x_prompt, g1, w_in_bf, tm=512, q_dtype=BF16, bf16_kv=True)
    o = _prompt_attention(lamp, sg, q, k_bf, v_bf, tq=256, tk=512)
    s, h_p = _ssm_glu(u, jnp.zeros((n_batch, 2 * N_STATE), F32), *ssm_w, steps_per_block=64)

    xs = x_sample.reshape(1, n_dec, D_MODEL)
    qs, ks, vs, us = _in_proj(xs, g1, w_in_bf, tm=n_dec, q_dtype=F32, bf16_kv=False)
    seq_rows = lambda t: t.reshape(n_dec, D_QK)
    y_p, conv_p, o_s = _out_ffn_with_decode(
        x_prompt, o, s, jnp.zeros((n_batch, 2, D_FF), F32), ffn_w,
        page_table, lamp, sg, seq_rows(qs), seq_rows(ks), seq_rows(vs),
        cache_k.transpose(0, 1, 3, 4, 5, 2).reshape(n_pool, D_QK, PAGE),
        cache_v.reshape(n_pool, PAGE * N_HEADS, V_DIM),
        tm=256, pages_per_chunk=16)
    h0_s = jnp.concatenate([state_ssm_re[0].reshape(n_dec, N_STATE),
                            state_ssm_im[0].reshape(n_dec, N_STATE)], axis=-1)
    s_s, h_s = _ssm_glu(us.reshape(n_dec, D_SSM), h0_s, *ssm_w, steps_per_block=1)
    y_s, conv_s = _out_ffn(xs, o_s.reshape(1, n_dec, D_ATT).astype(BF16), s_s.reshape(1, n_dec, D_SSM),
                           cache_ffn_conv[0].reshape(1, n_dec, 2 * D_FF), *ffn_w,
                           tm=n_dec, rows_are_positions=False)

    heads_k = lambda t, lead: t.reshape(1, *lead, N_HEADS, 2, HEAD_DIM)
    heads_v = lambda t, lead: t.reshape(1, *lead, N_HEADS, V_DIM)
    state = lambda h, n: h.reshape(1, n, N_GROUPS, SSM_STATE)
    return (y_p, y_s.reshape(n_dec, 1, D_MODEL),
            heads_k(k, (n_batch, seq)), heads_v(v, (n_batch, seq)),
            heads_k(ks, (n_dec, 1)), heads_v(vs, (n_dec, 1)),
            state(h_p[:, :N_STATE], n_batch), state(h_p[:, N_STATE:], n_batch),
            state(h_s[:, :N_STATE], n_dec), state(h_s[:, N_STATE:], n_dec),
            conv_p.reshape(1, n_batch, 2, D_FF), conv_s.reshape(1, n_dec, 2, D_FF))
```
